```python
import jax, jax.numpy as jnp
from jax import lax
import numpy as np

D_MODEL = 1024
BATCH = 16
SEQ = 2048
DEPTH = 4

HEAD_DIM = 64
N_FOX_HEADS = 8
N_MOBA_HEADS = 8
FOX_WIDTH = N_FOX_HEADS * HEAD_DIM
MOBA_WIDTH = N_MOBA_HEADS * HEAD_DIM
ATTN_WIDTH = FOX_WIDTH + MOBA_WIDTH
FOX_Q_BLOCK = 128
MOBA_BLOCK = 256
MOBA_TOPK = 3
MOBA_Q_CHUNK = 16
POOL_EXPAND = 2
POOL_WIDTH = POOL_EXPAND * D_MODEL
POOL_WINDOWS = (2, 4, 8, 16)
POOL_GROUP = POOL_WIDTH // len(POOL_WINDOWS)
ATTN_SPLIT_SIZES = (FOX_WIDTH, FOX_WIDTH, FOX_WIDTH, FOX_WIDTH, N_FOX_HEADS,
                    MOBA_WIDTH, MOBA_WIDTH, MOBA_WIDTH, MOBA_WIDTH)
ATTN_IN = sum(ATTN_SPLIT_SIZES)
DEEPNORM_ALPHA = (2 * DEPTH) ** 0.25
DEEPNORM_BETA = (8 * DEPTH) ** -0.25
LN_EPS = 1e-5
N_ATTN_LAYERS = (DEPTH + 1) // 2
N_POOL_LAYERS = DEPTH // 2

kernel_name = "fox_moba_pool_deepnorm_hybrid"


def layer_norm(x, g, b):
    xf = x.astype(jnp.float32)
    mu = jnp.mean(xf, axis=-1, keepdims=True)
    var = jnp.mean(jnp.square(xf - mu), axis=-1, keepdims=True)
    y = (xf - mu) * lax.rsqrt(var + LN_EPS) * g.astype(jnp.float32) + b.astype(jnp.float32)
    return y.astype(x.dtype)


def split_heads(t, n_heads):
    B, S, _ = t.shape
    return t.reshape(B, S, n_heads, HEAD_DIM).transpose(0, 2, 1, 3)


def merge_heads(t):
    B, H, S, Dh = t.shape
    return t.transpose(0, 2, 1, 3).reshape(B, S, H * Dh)


def fox_attention(q, k, v, log_f):
    B, H, S, Dh = q.shape
    c = jnp.cumsum(log_f, axis=-1)
    nq = S // FOX_Q_BLOCK
    qb = q.reshape(B, H, nq, FOX_Q_BLOCK, Dh).transpose(2, 0, 1, 3, 4)
    cb = c.reshape(B, H, nq, FOX_Q_BLOCK).transpose(2, 0, 1, 3)
    key_pos = jnp.arange(S)
    scale = Dh ** -0.5

    def block(args):
        i, q_i, c_i = args
        s = jnp.einsum('bhqd,bhkd->bhqk', q_i, k, preferred_element_type=jnp.float32) * scale
        s = s + (c_i[..., :, None] - c[..., None, :])
        q_pos = i * FOX_Q_BLOCK + jnp.arange(FOX_Q_BLOCK)
        s = jnp.where(key_pos[None, :] <= q_pos[:, None], s, -jnp.inf)
        p = jax.nn.softmax(s, axis=-1)
        return jnp.einsum('bhqk,bhkd->bhqd', p.astype(v.dtype), v)

    out = lax.map(block, (jnp.arange(nq), qb, cb))
    return out.transpose(1, 2, 0, 3, 4).reshape(B, H, S, Dh)


def moba_attention(q, k, v):
    B, H, S, Dh = q.shape
    nb = -(-S // MOBA_BLOCK)
    pad = nb * MOBA_BLOCK - S
    kb = jnp.pad(k, ((0, 0), (0, 0), (0, pad), (0, 0))).reshape(B, H, nb, MOBA_BLOCK, Dh)
    vb = jnp.pad(v, ((0, 0), (0, 0), (0, pad), (0, 0))).reshape(B, H, nb, MOBA_BLOCK, Dh)
    k_mean = jnp.mean(kb.astype(jnp.float32), axis=3)
    gate = jnp.einsum('bhsd,bhnd->bhsn', q.astype(jnp.float32), k_mean)
    q_blk = jnp.arange(S) // MOBA_BLOCK
    fully_past = jnp.arange(nb)[None, :] < q_blk[:, None]
    gate = jnp.where(fully_past, gate, -jnp.inf)
    kk = min(MOBA_TOPK, nb)
    top_val, top_idx = lax.top_k(gate, kk)
    valid = jnp.isfinite(top_val)

    C = MOBA_Q_CHUNK
    nc = S // C
    qc = q.reshape(B, H, nc, C, Dh).transpose(2, 0, 1, 3, 4)
    idx_c = top_idx.reshape(B, H, nc, C, kk).transpose(2, 0, 1, 3, 4)
    val_c = valid.reshape(B, H, nc, C, kk).transpose(2, 0, 1, 3, 4)
    scale = Dh ** -0.5
    gather = jax.vmap(jax.vmap(lambda blocks, ids: blocks[ids]))

    def chunk(args):
        i, q_i, idx_i, valid_i = args
        start = i * C
        blk = start // MOBA_BLOCK
        k_own = lax.dynamic_index_in_dim(kb, blk, axis=2, keepdims=False)
        v_own = lax.dynamic_index_in_dim(vb, blk, axis=2, keepdims=False)
        s_own = jnp.einsum('bhqd,bhkd->bhqk', q_i, k_own, preferred_element_type=jnp.float32) * scale
        q_pos = start + jnp.arange(C)
        k_pos = blk * MOBA_BLOCK + jnp.arange(MOBA_BLOCK)
        s_own = jnp.where(k_pos[None, :] <= q_pos[:, None], s_own, -jnp.inf)
        k_sel = gather(kb, idx_i)
        v_sel = gather(vb, idx_i)
        s_sel = jnp.einsum('bhqd,bhqnkd->bhqnk', q_i, k_sel, preferred_element_type=jnp.float32) * scale
        s_sel = jnp.where(valid_i[..., None], s_sel, -jnp.inf).reshape(B, H, C, kk * MOBA_BLOCK)
        p = jax.nn.softmax(jnp.concatenate([s_sel, s_own], axis=-1), axis=-1).astype(v.dtype)
        p_sel = p[..., :kk * MOBA_BLOCK].reshape(B, H, C, kk, MOBA_BLOCK)
        p_own = p[..., kk * MOBA_BLOCK:]
        return (jnp.einsum('bhqnk,bhqnkd->bhqd', p_sel, v_sel)
                + jnp.einsum('bhqk,bhkd->bhqd', p_own, v_own))

    out = lax.map(chunk, (jnp.arange(nc), qc, idx_c, val_c))
    return out.transpose(1, 2, 0, 3, 4).reshape(B, H, S, Dh)


def causal_multiscale_pool(u):
    B, S, W = u.shape
    uf = u.astype(jnp.float32)
    cs = jnp.pad(jnp.cumsum(uf, axis=1), ((0, 0), (1, 0), (0, 0)))
    t = jnp.arange(S)
    outs = []
    for g, w in enumerate(POOL_WINDOWS):
        cs_g = cs[..., g * POOL_GROUP:(g + 1) * POOL_GROUP]
        lo = jnp.maximum(t + 1 - w, 0)
        cnt = jnp.minimum(t + 1, w).astype(jnp.float32)
        outs.append((cs_g[:, t + 1] - cs_g[:, lo]) / cnt[None, :, None])
    return (jnp.concatenate(outs, axis=-1) - uf).astype(u.dtype)


def attn_sublayer(x, w_in, b_f, w_out):
    h = x @ w_in
    offsets = [int(o) for o in np.cumsum(ATTN_SPLIT_SIZES)[:-1]]
    fq, fk, fv, fg, ff, mq, mk, mv, mg = jnp.split(h, offsets, axis=-1)
    log_f = jax.nn.log_sigmoid((ff + b_f).astype(jnp.float32)).transpose(0, 2, 1)
    y_fox = fox_attention(split_heads(fq, N_FOX_HEADS), split_heads(fk, N_FOX_HEADS),
                          split_heads(fv, N_FOX_HEADS), log_f)
    y_moba = moba_attention(split_heads(mq, N_MOBA_HEADS), split_heads(mk, N_MOBA_HEADS),
                            split_heads(mv, N_MOBA_HEADS))
    y = jnp.concatenate([merge_heads(y_fox) * jax.nn.silu(fg),
                         merge_heads(y_moba) * jax.nn.silu(mg)], axis=-1)
    return y @ w_out


def pool_sublayer(x, w_in, w_grp, scale, w_out):
    B, S, _ = x.shape
    h = x @ w_in
    u, gate = h[..., :POOL_WIDTH], h[..., POOL_WIDTH:]
    pooled = causal_multiscale_pool(u).reshape(B, S, len(POOL_WINDOWS), POOL_GROUP)
    y = jnp.einsum('bsgc,gcd->bsgd', pooled, w_grp).reshape(B, S, POOL_WIDTH) * scale
    return (y * jax.nn.silu(gate)) @ w_out


def setup_inputs(seed: int = 0) -> dict:
    key = jax.random.key(seed)
    ks = jax.random.split(key, 12)
    f32 = jnp.float32
    x = jax.random.normal(ks[0], (BATCH, SEQ, D_MODEL), f32)
    attn_w_in = jax.random.normal(ks[1], (N_ATTN_LAYERS, D_MODEL, ATTN_IN), f32) * D_MODEL ** -0.5
    attn_b_f = jax.random.uniform(ks[2], (N_ATTN_LAYERS, N_FOX_HEADS), f32, 1.0, 4.0)
    attn_w_out = (jax.random.normal(ks[3], (N_ATTN_LAYERS, ATTN_WIDTH, D_MODEL), f32)
                  * ATTN_WIDTH ** -0.5 * DEEPNORM_BETA)
    pool_w_in = jax.random.normal(ks[4], (N_POOL_LAYERS, D_MODEL, 2 * POOL_WIDTH), f32) * D_MODEL ** -0.5
    pool_w_grp = (jax.random.normal(ks[5], (N_POOL_LAYERS, len(POOL_WINDOWS), POOL_GROUP, POOL_GROUP), f32)
                  * POOL_GROUP ** -0.5)
    pool_scale = 1.0 + 0.1 * jax.random.normal(ks[6], (N_POOL_LAYERS, POOL_WIDTH), f32)
    pool_w_out = (jax.random.normal(ks[7], (N_POOL_LAYERS, POOL_WIDTH, D_MODEL), f32)
                  * POOL_WIDTH ** -0.5 * DEEPNORM_BETA)
    ln_g = 1.0 + 0.02 * jax.random.normal(ks[8], (DEPTH, D_MODEL), f32)
    ln_b = 0.02 * jax.random.normal(ks[9], (DEPTH, D_MODEL), f32)
    return {"x": x, "attn_w_in": attn_w_in, "attn_b_f": attn_b_f, "attn_w_out": attn_w_out,
            "pool_w_in": pool_w_in, "pool_w_grp": pool_w_grp, "pool_scale": pool_scale,
            "pool_w_out": pool_w_out, "ln_g": ln_g, "ln_b": ln_b}


def reference(x, attn_w_in, attn_b_f, attn_w_out, pool_w_in, pool_w_grp, pool_scale,
              pool_w_out, ln_g, ln_b):
    for layer in range(DEPTH):
        j = layer // 2
        if layer % 2 == 0:
            f = attn_sublayer(x, attn_w_in[j], attn_b_f[j], attn_w_out[j])
        else:
            f = pool_sublayer(x, pool_w_in[j], pool_w_grp[j], pool_scale[j], pool_w_out[j])
        x = layer_norm(DEEPNORM_ALPHA * x + f, ln_g[layer], ln_b[layer])
    return x
```

```python
import functools

import jax
import jax.numpy as jnp
import numpy as np
from jax import lax
from jax.experimental import pallas as pl
from jax.experimental.pallas import tpu as pltpu

D_MODEL = 1024
DEPTH = 4
HEAD_DIM = 64
N_HEADS = 8
HEAD_WIDTH = N_HEADS * HEAD_DIM
N_PAIRS = N_HEADS // 2
MOBA_BLOCK = 256
MOBA_TOPK = 3
POOL_WIDTH = 2048
POOL_WINDOWS = (2, 4, 8, 16)
POOL_GROUP = 512
POOL_HALO = 16
DEEPNORM_ALPHA = (2 * DEPTH) ** 0.25
LN_EPS = 1e-5
QK_SCALE = HEAD_DIM ** -0.5

LANES = 128
MASK_BIAS = -30000.0
VMEM_LIMIT = 56 * 1024 * 1024

BF16 = jnp.bfloat16
F32 = jnp.float32


def _dot(a, b):
    return jnp.dot(a, b, preferred_element_type=F32)


def _dot_nt(a, b):
    return lax.dot_general(a, b, (((1,), (1,)), ((), ())), preferred_element_type=F32)


def _split3(v):
    hi = v.astype(BF16)
    r1 = v - hi.astype(F32)
    mid = r1.astype(BF16)
    lo = (r1 - mid.astype(F32)).astype(BF16)
    return hi, mid, lo


def _layer_norm(z, g, b):
    mu = jnp.mean(z, axis=-1, keepdims=True)
    zc = z - mu
    var = jnp.mean(zc * zc, axis=-1, keepdims=True)
    return zc * lax.rsqrt(var + LN_EPS) * g + b


def _silu(v):
    return v * jax.nn.sigmoid(v)


def _attn_in_kernel(x_ref, w_ref, wff_ref, bf_ref, place_ref, const_ref,
                    fq_ref, fk_ref, fv_ref, fg_ref, mq_ref, mk_ref, mv_ref, mg_ref,
                    carry_ref, *, tm):
    s = pl.program_id(1)
    xb = x_ref[0].astype(BF16)
    h = _dot(xb, w_ref[...])
    ff = _dot(xb, wff_ref[...]) + bf_ref[...]
    log_f = jax.nn.log_sigmoid(ff)

    @pl.when(s == 0)
    def _():
        carry_ref[...] = jnp.zeros_like(carry_ref)

    row = lax.broadcasted_iota(jnp.int32, (tm, tm), 0)
    col = lax.broadcasted_iota(jnp.int32, (tm, tm), 1)
    tri = jnp.where(col <= row, 1.0, 0.0).astype(BF16)
    hi, mid, lo = _split3(log_f)
    c = (_dot(tri, hi) + _dot(tri, mid)) + _dot(tri, lo) + carry_ref[0:1, :]
    carry_ref[0:1, :] = c[tm - 1:tm, :]

    c_hi, c_mid, c_lo = _split3(c)
    ext = _dot(jnp.concatenate([c_hi, c_mid, c_lo], axis=-1), place_ref[...]) + const_ref[...]

    W = HEAD_WIDTH
    for p in range(N_PAIRS):
        sl = slice(p * LANES, (p + 1) * LANES)
        fq_ref[0, :, 2 * p * LANES:(2 * p + 1) * LANES] = h[:, sl].astype(BF16)
        fq_ref[0, :, (2 * p + 1) * LANES:(2 * p + 2) * LANES] = ext[:, sl].astype(BF16)
        fk_ref[0, :, 2 * p * LANES:(2 * p + 1) * LANES] = h[:, W + p * LANES:W + (p + 1) * LANES].astype(BF16)
        fk_ref[0, :, (2 * p + 1) * LANES:(2 * p + 2) * LANES] = ext[:, W + p * LANES:W + (p + 1) * LANES].astype(BF16)
    fv_ref[0] = h[:, 2 * W:3 * W].astype(BF16)
    fg_ref[0] = _silu(h[:, 3 * W:4 * W]).astype(BF16)
    mq_ref[0] = h[:, 4 * W:5 * W].astype(BF16)
    mk_ref[0] = h[:, 5 * W:6 * W].astype(BF16)
    mv_ref[0] = h[:, 6 * W:7 * W].astype(BF16)
    mg_ref[0] = _silu(h[:, 7 * W:8 * W]).astype(BF16)


def _decay_placement():
    place = np.zeros((3 * LANES, 2 * HEAD_WIDTH), np.float32)
    const = np.zeros((1, 2 * HEAD_WIDTH), np.float32)
    for h in range(N_HEADS):
        p, a = divmod(h, 2)
        base_q = p * LANES + a * HEAD_DIM
        base_k = HEAD_WIDTH + p * LANES + a * HEAD_DIM
        for r in range(3):
            place[r * LANES + h, base_q + r] = 1.0
            const[0, base_q + 3 + r] = 1.0
            const[0, base_k + r] = 1.0
            place[r * LANES + h, base_k + 3 + r] = -1.0
    return jnp.asarray(place, BF16), jnp.asarray(const, F32)


def _attn_in_proj(x, w_main, w_ff, b_f, *, tm=256):
    B, S, D = x.shape
    place, const = _decay_placement()
    W = HEAD_WIDTH
    full = lambda shape: pl.BlockSpec(shape, lambda b, s: (0,) * len(shape))
    tile = lambda width: pl.BlockSpec((1, tm, width), lambda b, s: (b, s, 0))
    widths = (2 * W, 2 * W, W, W, W, W, W, W)
    return pl.pallas_call(
        functools.partial(_attn_in_kernel, tm=tm),
        grid=(B, S // tm),
        in_specs=[tile(D), full(w_main.shape), full(w_ff.shape), full(b_f.shape),
                  full(place.shape), full(const.shape)],
        out_specs=[tile(w) for w in widths],
        out_shape=[jax.ShapeDtypeStruct((B, S, w), BF16) for w in widths],
        scratch_shapes=[pltpu.VMEM((8, LANES), F32)],
        compiler_params=pltpu.CompilerParams(
            dimension_semantics=("arbitrary", "arbitrary"), vmem_limit_bytes=VMEM_LIMIT),
        name="attn_in_proj",
    )(x, w_main, w_ff, b_f, place, const)


def _softmax_step(s, v, m, l, acc):
    m_new = jnp.maximum(m, jnp.max(s, axis=-1, keepdims=True))
    alpha = jnp.exp(m - m_new)
    p = jnp.exp(s - m_new)
    l = alpha * l + jnp.sum(p, axis=-1, keepdims=True)
    acc = alpha * acc + _dot(p.astype(BF16), v)
    return m_new, l, acc


def _head_lane_mask(shape, a):
    lane = lax.broadcasted_iota(jnp.int32, shape, 1)
    return (lane // HEAD_DIM) % 2 == a


def _fox_kernel(q_ref, k_ref, v_ref, g_ref, o_ref, *, tq):
    qi = pl.program_id(2)
    q2 = q_ref[0]
    row = lax.broadcasted_iota(jnp.int32, (tq, tq), 0)
    col = lax.broadcasted_iota(jnp.int32, (tq, tq), 1)
    causal = col <= row
    outs = []
    for a in range(2):
        qa = jnp.where(_head_lane_mask(q2.shape, a), q2, jnp.zeros_like(q2))

        def body(j, carry, qa=qa):
            start = pl.multiple_of(j * tq, tq)
            s = _dot_nt(qa, k_ref[0, pl.ds(start, tq), :])
            return _softmax_step(s, v_ref[0, pl.ds(start, tq), :], *carry)

        init = (jnp.full((tq, 1), -jnp.inf, F32), jnp.zeros((tq, 1), F32), jnp.zeros((tq, LANES), F32))
        carry = lax.fori_loop(0, qi, body, init)
        start = pl.multiple_of(qi * tq, tq)
        s = _dot_nt(qa, k_ref[0, pl.ds(start, tq), :])
        s = jnp.where(causal, s, -jnp.inf)
        _, l, acc = _softmax_step(s, v_ref[0, pl.ds(start, tq), :], *carry)
        outs.append(acc / l)
    lane = lax.broadcasted_iota(jnp.int32, (tq, LANES), 1)
    y = jnp.where(lane < HEAD_DIM, outs[0], outs[1]) * g_ref[0].astype(F32)
    o_ref[0] = y.astype(BF16)


def _fox_attention(q_aug, k_aug, v, g, *, tq=256):
    B, S, _ = v.shape
    return pl.pallas_call(
        functools.partial(_fox_kernel, tq=tq),
        grid=(B, N_PAIRS, S // tq),
        in_specs=[pl.BlockSpec((1, tq, 2 * LANES), lambda b, p, i: (b, i, p)),
                  pl.BlockSpec((1, S, 2 * LANES), lambda b, p, i: (b, 0, p)),
                  pl.BlockSpec((1, S, LANES), lambda b, p, i: (b, 0, p)),
                  pl.BlockSpec((1, tq, LANES), lambda b, p, i: (b, i, p))],
        out_specs=pl.BlockSpec((1, tq, LANES), lambda b, p, i: (b, i, p)),
        out_shape=jax.ShapeDtypeStruct((B, S, HEAD_WIDTH), BF16),
        compiler_params=pltpu.CompilerParams(
            dimension_semantics=("arbitrary", "arbitrary", "arbitrary"), vmem_limit_bytes=VMEM_LIMIT),
        name="fox_attention",
    )(q_aug, k_aug, v, g)


def _moba_kernel(q_ref, k_ref, v_ref, g_ref, o_ref, kmean_ref, *, seq):
    tq = MOBA_BLOCK
    nb = seq // tq
    qi = pl.program_id(2)

    @pl.when(qi == 0)
    def _():
        r = lax.broadcasted_iota(jnp.int32, (LANES, seq), 0)
        t = lax.broadcasted_iota(jnp.int32, (LANES, seq), 1)
        avg = jnp.where(t // tq == r, 1.0 / tq, 0.0).astype(BF16)
        kmean_ref[...] = _dot(avg, k_ref[0])

    q2 = q_ref[0]
    kmean = kmean_ref[...].astype(BF16)
    row = lax.broadcasted_iota(jnp.int32, (tq, tq), 0)
    col = lax.broadcasted_iota(jnp.int32, (tq, tq), 1)
    causal = col <= row
    lane = lax.broadcasted_iota(jnp.int32, (tq, LANES), 1)
    outs = []
    for a in range(2):
        qa = jnp.where(_head_lane_mask(q2.shape, a), q2, jnp.zeros_like(q2))
        gate = jnp.where(lane < qi, _dot_nt(qa, kmean), -jnp.inf)
        rank = jnp.zeros((tq, LANES), jnp.int32)
        for d in range(1, nb):
            lower = pltpu.roll(gate, d, 1)
            upper = pltpu.roll(gate, LANES - d, 1)
            rank = rank + (lower >= gate).astype(jnp.int32) + (upper > gate).astype(jnp.int32)
        keep = (rank < MOBA_TOPK) & (gate > -jnp.inf)
        bias = jnp.where(keep, 0.0, MASK_BIAS).astype(BF16)
        qa_aug = jnp.concatenate([qa, bias], axis=-1)

        start = pl.multiple_of(qi * tq, tq)
        s = _dot_nt(qa, k_ref[0, pl.ds(start, tq), :])
        s = jnp.where(causal, s, -jnp.inf)
        init = (jnp.full((tq, 1), -jnp.inf, F32), jnp.zeros((tq, 1), F32), jnp.zeros((tq, LANES), F32))
        carry = _softmax_step(s, v_ref[0, pl.ds(start, tq), :], *init)

        def body(j, carry, qa_aug=qa_aug):
            start = pl.multiple_of(j * tq, tq)
            onehot = jnp.where(lane == j, 1.0, 0.0).astype(BF16)
            k_aug = jnp.concatenate([k_ref[0, pl.ds(start, tq), :], onehot], axis=-1)
            s = _dot_nt(qa_aug, k_aug)
            return _softmax_step(s, v_ref[0, pl.ds(start, tq), :], *carry)

        _, l, acc = lax.fori_loop(0, qi, body, carry)
        outs.append(acc / l)
    y = jnp.where(lane < HEAD_DIM, outs[0], outs[1]) * g_ref[0].astype(F32)
    o_ref[0] = y.astype(BF16)


def _moba_attention(q, k, v, g):
    B, S, _ = v.shape
    tq = MOBA_BLOCK
    return pl.pallas_call(
        functools.partial(_moba_kernel, seq=S),
        grid=(B, N_PAIRS, S // tq),
        in_specs=[pl.BlockSpec((1, tq, LANES), lambda b, p, i: (b, i, p)),
                  pl.BlockSpec((1, S, LANES), lambda b, p, i: (b, 0, p)),
                  pl.BlockSpec((1, S, LANES), lambda b, p, i: (b, 0, p)),
                  pl.BlockSpec((1, tq, LANES), lambda b, p, i: (b, i, p))],
        out_specs=pl.BlockSpec((1, tq, LANES), lambda b, p, i: (b, i, p)),
        out_shape=jax.ShapeDtypeStruct((B, S, HEAD_WIDTH), BF16),
        scratch_shapes=[pltpu.VMEM((LANES, LANES), F32)],
        compiler_params=pltpu.CompilerParams(
            dimension_semantics=("arbitrary", "arbitrary", "arbitrary"), vmem_limit_bytes=VMEM_LIMIT),
        name="moba_attention",
    )(q, k, v, g)


def _attn_out_kernel(yf_ref, ym_ref, x_ref, wf_ref, wm_ref, g_ref, b_ref, o_ref):
    f = _dot(yf_ref[0], wf_ref[...]) + _dot(ym_ref[0], wm_ref[...])
    z = DEEPNORM_ALPHA * x_ref[0] + f
    o_ref[0] = _layer_norm(z, g_ref[...], b_ref[...])


def _attn_out_proj(y_fox, y_moba, x, w_f, w_m, ln_g, ln_b, *, tm=512):
    B, S, D = x.shape
    full = lambda shape: pl.BlockSpec(shape, lambda b, s: (0,) * len(shape))
    tile = lambda width: pl.BlockSpec((1, tm, width), lambda b, s: (b, s, 0))
    return pl.pallas_call(
        _attn_out_kernel,
        grid=(B, S // tm),
        in_specs=[tile(HEAD_WIDTH), tile(HEAD_WIDTH), tile(D), full(w_f.shape), full(w_m.shape),
                  full(ln_g.shape), full(ln_b.shape)],
        out_specs=tile(D),
        out_shape=jax.ShapeDtypeStruct((B, S, D), F32),
        compiler_params=pltpu.CompilerParams(
            dimension_semantics=("arbitrary", "arbitrary"), vmem_limit_bytes=VMEM_LIMIT),
        name="attn_out_proj",
    )(y_fox, y_moba, x, w_f, w_m, ln_g, ln_b)


def _pool_kernel(x_ref, win_ref, wgrp_ref, scale_ref, wout_ref, g_ref, b_ref, o_ref, ubuf_ref, *, tm):
    s = pl.program_id(1)
    x = x_ref[0]
    h = _dot(x.astype(BF16), win_ref[...])

    @pl.when(s == 0)
    def _():
        ubuf_ref[0:POOL_HALO, :] = jnp.zeros((POOL_HALO, POOL_WIDTH), F32)

    @pl.when(s > 0)
    def _():
        ubuf_ref[0:POOL_HALO, :] = ubuf_ref[tm:tm + POOL_HALO, :]

    ubuf_ref[POOL_HALO:, :] = h[:, :POOL_WIDTH]

    t = s * tm + lax.broadcasted_iota(jnp.int32, (tm, LANES), 0)
    f = jnp.zeros((tm, D_MODEL), F32)
    for g, w in enumerate(POOL_WINDOWS):
        cols = slice(g * POOL_GROUP, (g + 1) * POOL_GROUP)
        win = ubuf_ref[:, cols]
        span = 1
        while span < w:
            win = win + pltpu.roll(win, span, 0)
            span *= 2
        inv = 1.0 / jnp.minimum(t + 1, w).astype(F32)
        inv = jnp.concatenate([inv] * (POOL_GROUP // LANES), axis=-1)
        pooled = win[POOL_HALO:, :] * inv - h[:, cols]
        y = _dot(pooled.astype(BF16), wgrp_ref[g]) * scale_ref[:, cols]
        gate = h[:, POOL_WIDTH + g * POOL_GROUP:POOL_WIDTH + (g + 1) * POOL_GROUP]
        f = f + _dot((y * _silu(gate)).astype(BF16), wout_ref[cols, :])
    z = DEEPNORM_ALPHA * x + f
    o_ref[0] = _layer_norm(z, g_ref[...], b_ref[...])


def _pool_layer(x, w_in, w_grp, scale, w_out, ln_g, ln_b, *, tm=256):
    B, S, D = x.shape
    full = lambda shape: pl.BlockSpec(shape, lambda b, s: (0,) * len(shape))
    tile = pl.BlockSpec((1, tm, D), lambda b, s: (b, s, 0))
    return pl.pallas_call(
        functools.partial(_pool_kernel, tm=tm),
        grid=(B, S // tm),
        in_specs=[tile, full(w_in.shape), full(w_grp.shape), full(scale.shape), full(w_out.shape),
                  full(ln_g.shape), full(ln_b.shape)],
        out_specs=tile,
        out_shape=jax.ShapeDtypeStruct((B, S, D), F32),
        scratch_shapes=[pltpu.VMEM((POOL_HALO + tm, POOL_WIDTH), F32)],
        compiler_params=pltpu.CompilerParams(
            dimension_semantics=("arbitrary", "arbitrary"), vmem_limit_bytes=VMEM_LIMIT),
        name="pool_layer",
    )(x, w_in, w_grp, scale, w_out, ln_g, ln_b)


def _attn_layer(x, w_in, b_f, w_out, ln_g, ln_b):
    W = HEAD_WIDTH
    n_ff = N_HEADS
    w_main = jnp.concatenate([w_in[:, :W] * QK_SCALE, w_in[:, W:4 * W],
                              w_in[:, 4 * W + n_ff:5 * W + n_ff] * QK_SCALE, w_in[:, 5 * W + n_ff:]],
                             axis=1).astype(BF16)
    w_ff = jnp.pad(w_in[:, 4 * W:4 * W + n_ff], ((0, 0), (0, LANES - n_ff))).astype(BF16)
    b_pad = jnp.pad(b_f, (0, LANES - n_ff)).reshape(1, LANES)
    fq, fk, fv, fg, mq, mk, mv, mg = _attn_in_proj(x, w_main, w_ff, b_pad)
    y_fox = _fox_attention(fq, fk, fv, fg)
    y_moba = _moba_attention(mq, mk, mv, mg)
    return _attn_out_proj(y_fox, y_moba, x, w_out[:W].astype(BF16), w_out[W:].astype(BF16),
                          ln_g.reshape(1, -1), ln_b.reshape(1, -1))


def kernel(x, attn_w_in, attn_b_f, attn_w_out, pool_w_in, pool_w_grp, pool_scale, pool_w_out, ln_g, ln_b):
    for layer in range(DEPTH):
        j = layer // 2
        if layer % 2 == 0:
            x = _attn_layer(x, attn_w_in[j], attn_b_f[j], attn_w_out[j], ln_g[layer], ln_b[layer])
        else:
            x = _pool_layer(x, pool_w_in[j].astype(BF16), pool_w_grp[j].astype(BF16),
                            pool_scale[j].reshape(1, -1), pool_w_out[j].astype(BF16),
                            ln_g[layer].reshape(1, -1), ln_b[layer].reshape(1, -1))
    return x
```

```python
import functools
import math

import jax
import jax.numpy as jnp
import numpy as np
from jax import lax
from jax.experimental import pallas as pl
from jax.experimental.pallas import tpu as pltpu

D_MODEL = 1024
DEPTH = 4
HEAD_DIM = 64
N_HEADS = 8
HEAD_WIDTH = N_HEADS * HEAD_DIM
N_PAIRS = N_HEADS // 2
MOBA_BLOCK = 256
MOBA_TOPK = 3
POOL_WIDTH = 2048
POOL_WINDOWS = (2, 4, 8, 16)
POOL_GROUP = 512
POOL_HALO = 16
DEEPNORM_ALPHA = (2 * DEPTH) ** 0.25
LN_EPS = 1e-5
LOG2E = math.log2(math.e)
Q_SCALE = HEAD_DIM ** -0.5 * LOG2E

LANES = 128
SUBLANES = 8
MASK_BIAS = -30000.0
VMEM_LIMIT = 56 * 1024 * 1024

BF16 = jnp.bfloat16
F32 = jnp.float32


def _dot(a, b):
    return jnp.dot(a, b, preferred_element_type=F32)


def _dot_nt(a, b):
    return lax.dot_general(a, b, (((1,), (1,)), ((), ())), preferred_element_type=F32)


def _split3(v):
    hi = v.astype(BF16)
    r1 = v - hi.astype(F32)
    mid = r1.astype(BF16)
    lo = (r1 - mid.astype(F32)).astype(BF16)
    return hi, mid, lo


def _layer_norm(z, g, b):
    mu = jnp.mean(z, axis=-1, keepdims=True)
    zc = z - mu
    var = jnp.mean(zc * zc, axis=-1, keepdims=True)
    return zc * lax.rsqrt(var + LN_EPS) * g + b


def _silu(v):
    return v * jax.nn.sigmoid(v)


def _attn_in_kernel(x_ref, w_ref, wff_ref, bf_ref, place_ref, const_ref,
                    fq_ref, fk_ref, fv_ref, fg_ref, mq_ref, mk_ref, mv_ref, mg_ref,
                    carry_ref, *, tm):
    s = pl.program_id(1)
    xb = x_ref[0].astype(BF16)
    h = _dot(xb, w_ref[...])
    ff = _dot(xb, wff_ref[...]) + bf_ref[...]
    log_f = jax.nn.log_sigmoid(ff)

    @pl.when(s == 0)
    def _():
        carry_ref[...] = jnp.zeros_like(carry_ref)

    row = lax.broadcasted_iota(jnp.int32, (tm, tm), 0)
    col = lax.broadcasted_iota(jnp.int32, (tm, tm), 1)
    tri = jnp.where(col <= row, 1.0, 0.0).astype(BF16)
    hi, mid, lo = _split3(log_f)
    c = (_dot(tri, hi) + _dot(tri, mid)) + _dot(tri, lo) + carry_ref[0:1, :]
    carry_ref[0:1, :] = c[tm - 1:tm, :]

    c_hi, c_mid, c_lo = _split3(c * LOG2E)
    ext = _dot(jnp.concatenate([c_hi, c_mid, c_lo], axis=-1), place_ref[...]) + const_ref[...]

    blk = (s * tm + lax.broadcasted_iota(jnp.int32, (tm, LANES), 0)) // MOBA_BLOCK
    onehot = jnp.where(lax.broadcasted_iota(jnp.int32, (tm, LANES), 1) == blk, 1.0, 0.0).astype(BF16)

    W = HEAD_WIDTH
    head_rows = lax.broadcasted_iota(jnp.int32, (LANES, tm), 0) < HEAD_DIM
    for p in range(N_PAIRS):
        sl = slice(p * LANES, (p + 1) * LANES)
        lo_lanes = slice(2 * p * LANES, (2 * p + 1) * LANES)
        hi_lanes = slice((2 * p + 1) * LANES, (2 * p + 2) * LANES)
        fq_ref[0, :, lo_lanes] = h[:, sl].astype(BF16)
        fq_ref[0, :, hi_lanes] = ext[:, sl].astype(BF16)
        fk_ref[0, :, lo_lanes] = h[:, W + p * LANES:W + (p + 1) * LANES].astype(BF16)
        fk_ref[0, :, hi_lanes] = ext[:, W + p * LANES:W + (p + 1) * LANES].astype(BF16)
        mk_ref[0, :, lo_lanes] = h[:, 5 * W + p * LANES:5 * W + (p + 1) * LANES].astype(BF16)
        mk_ref[0, :, hi_lanes] = onehot
        for v_ref, base in ((fv_ref, 2 * W), (mv_ref, 6 * W)):
            vt = h[:, base + p * LANES:base + (p + 1) * LANES].T
            v_ref[0, lo_lanes, :] = jnp.where(head_rows, vt, 1.0).astype(BF16)
            v_ref[0, hi_lanes, :] = jnp.where(head_rows, 1.0, vt).astype(BF16)
    fg_ref[0] = _silu(h[:, 3 * W:4 * W]).astype(BF16)
    mq_ref[0] = h[:, 4 * W:5 * W].astype(BF16)
    mg_ref[0] = _silu(h[:, 7 * W:8 * W]).astype(BF16)


def _decay_placement():
    place = np.zeros((3 * LANES, 2 * HEAD_WIDTH), np.float32)
    const = np.zeros((1, 2 * HEAD_WIDTH), np.float32)
    for h in range(N_HEADS):
        p, a = divmod(h, 2)
        base_q = p * LANES + a * HEAD_DIM
        base_k = HEAD_WIDTH + p * LANES + a * HEAD_DIM
        for r in range(3):
            place[r * LANES + h, base_q + r] = 1.0
            const[0, base_q + 3 + r] = 1.0
            const[0, base_k + r] = 1.0
            place[r * LANES + h, base_k + 3 + r] = -1.0
    return jnp.asarray(place, BF16), jnp.asarray(const, F32)


def _attn_in_proj(x, w_main, w_ff, b_f, *, tm=256):
    B, S, D = x.shape
    place, const = _decay_placement()
    W = HEAD_WIDTH
    full = lambda shape: pl.BlockSpec(shape, lambda b, s: (0,) * len(shape))
    tile = lambda width: pl.BlockSpec((1, tm, width), lambda b, s: (b, s, 0))
    row_major = lambda w: (tile(w), jax.ShapeDtypeStruct((B, S, w), BF16))
    transposed = (pl.BlockSpec((1, 2 * W, tm), lambda b, s: (b, 0, s)), jax.ShapeDtypeStruct((B, 2 * W, S), BF16))
    outs = (row_major(2 * W), row_major(2 * W), transposed, row_major(W),
            row_major(W), row_major(2 * W), transposed, row_major(W))
    return pl.pallas_call(
        functools.partial(_attn_in_kernel, tm=tm),
        grid=(B, S // tm),
        in_specs=[tile(D), full(w_main.shape), full(w_ff.shape), full(b_f.shape),
                  full(place.shape), full(const.shape)],
        out_specs=[o[0] for o in outs],
        out_shape=[o[1] for o in outs],
        scratch_shapes=[pltpu.VMEM((8, LANES), F32)],
        compiler_params=pltpu.CompilerParams(
            dimension_semantics=("arbitrary", "arbitrary"), vmem_limit_bytes=VMEM_LIMIT),
        name="attn_in_proj",
    )(x, w_main, w_ff, b_f, place, const)


def _head_row_mask(shape, a):
    row = lax.broadcasted_iota(jnp.int32, shape, 0)
    return (row // HEAD_DIM) % 2 == a


def _moba_bias_t(qa_t, kmean, qi, nb):
    tq = qa_t.shape[1]
    gate = _dot(kmean, qa_t)[0:SUBLANES, :]
    blk = lax.broadcasted_iota(jnp.int32, gate.shape, 0)
    gate = jnp.where(blk < qi, gate, -jnp.inf)
    rank = jnp.zeros(gate.shape, jnp.int32)
    for d in range(1, nb):
        other = pltpu.roll(gate, d, 0)
        rank = rank + jnp.where(blk >= d, jnp.where(other >= gate, 1, 0), jnp.where(other > gate, 1, 0))
    keep = ((rank < MOBA_TOPK) & (gate > -jnp.inf)) | (blk == qi)
    bias_t = jnp.where(keep, 0.0, MASK_BIAS)
    return jnp.concatenate([bias_t, jnp.zeros((LANES - SUBLANES, tq), F32)], axis=0).astype(BF16)


def _attn_core_kernel(fq_ref, fk_ref, fv_ref, fg_ref, mq_ref, mk_ref, mv_ref, mg_ref, y_ref,
                      kmean_ref, *, seq):
    tq = MOBA_BLOCK
    nb = seq // tq
    assert nb == SUBLANES
    qi = pl.program_id(2)

    @pl.when(qi == 0)
    def _():
        r = lax.broadcasted_iota(jnp.int32, (LANES, seq), 0)
        t = lax.broadcasted_iota(jnp.int32, (LANES, seq), 1)
        avg = jnp.where(t // tq == r, 1.0 / tq, 0.0).astype(BF16)
        kmean_ref[...] = _dot(avg, mk_ref[0, :, 0:LANES])

    fq_t = fq_ref[0].astype(F32).T.astype(BF16)
    mq_t = mq_ref[0].astype(F32).T.astype(BF16)
    kmean = kmean_ref[...].astype(BF16)

    chains = []
    for a in range(2):
        qa_t = jnp.where(_head_row_mask(fq_t.shape, a), fq_t, jnp.zeros_like(fq_t))
        chains.append((qa_t, fk_ref, fv_ref, a * LANES))
    for a in range(2):
        qa_t = jnp.where(_head_row_mask(mq_t.shape, a), mq_t, jnp.zeros_like(mq_t))
        qa_t = jnp.concatenate([qa_t, _moba_bias_t(qa_t, kmean, qi, nb)], axis=0)
        chains.append((qa_t, mk_ref, mv_ref, a * LANES))

    def scores_and_probs(start, ms, mask=None):
        scores = []
        for qa_t, k_ref, _, _ in chains:
            s_t = _dot(k_ref[0, pl.ds(start, tq), :], qa_t)
            if mask is not None:
                s_t = jnp.where(mask, s_t, -jnp.inf)
            scores.append(s_t)
        probs = []
        for s_t, m in zip(scores, ms):
            m_new = jnp.maximum(m, jnp.max(s_t, axis=0, keepdims=True))
            probs.append((m_new, jnp.exp2(m - m_new), jnp.exp2(s_t - m_new).astype(BF16)))
        return probs

    def accumulate(start, probs, accs):
        out = []
        for (_, _, v_ref, v_row), (_, alpha, p_t), acc in zip(chains, probs, accs):
            v_t = v_ref[0, v_row:v_row + LANES, pl.ds(start, tq)]
            out.append(alpha * acc + _dot(v_t, p_t))
        return out

    key = lax.broadcasted_iota(jnp.int32, (tq, tq), 0)
    qry = lax.broadcasted_iota(jnp.int32, (tq, tq), 1)
    diag = pl.multiple_of(qi * tq, tq)
    probs = scores_and_probs(diag, [jnp.full((1, tq), -jnp.inf, F32)] * 4, mask=key <= qry)
    accs = [jnp.zeros((LANES, tq), F32)] * 4

    def body(j, carry):
        probs, accs = carry
        prev = pl.multiple_of(jnp.where(j == 0, qi, j - 1) * tq, tq)
        new_probs = scores_and_probs(pl.multiple_of(j * tq, tq), [m for (m, _, _) in probs])
        return new_probs, accumulate(prev, probs, accs)

    probs, accs = lax.fori_loop(0, qi, body, (probs, accs))
    last = pl.multiple_of(jnp.where(qi == 0, qi, qi - 1) * tq, tq)
    accs = accumulate(last, probs, accs)

    head0_rows = lax.broadcasted_iota(jnp.int32, (LANES, tq), 0) < HEAD_DIM
    for kind, g_ref in enumerate((fg_ref, mg_ref)):
        o = [acc / pltpu.roll(acc, HEAD_DIM, 0) for acc in accs[2 * kind:2 * kind + 2]]
        y = jnp.where(head0_rows, o[0], o[1]).T
        y_ref[0, kind] = (y * g_ref[0].astype(F32)).astype(BF16)


def _attn_core(fq, fk, fv, fg, mq, mk, mv, mg):
    B, S, _ = fq.shape
    tq = MOBA_BLOCK
    q_tile = lambda width: pl.BlockSpec((1, tq, width), lambda b, p, i: (b, i, p))
    k_full = pl.BlockSpec((1, S, 2 * LANES), lambda b, p, i: (b, 0, p))
    v_full = pl.BlockSpec((1, 2 * LANES, S), lambda b, p, i: (b, p, 0))
    return pl.pallas_call(
        functools.partial(_attn_core_kernel, seq=S),
        grid=(B, N_PAIRS, S // tq),
        in_specs=[q_tile(2 * LANES), k_full, v_full, q_tile(LANES),
                  q_tile(LANES), k_full, v_full, q_tile(LANES)],
        out_specs=pl.BlockSpec((1, 2, tq, LANES), lambda b, p, i: (b, 0, i, p)),
        out_shape=jax.ShapeDtypeStruct((B, 2, S, HEAD_WIDTH), BF16),
        scratch_shapes=[pltpu.VMEM((LANES, LANES), F32)],
        compiler_params=pltpu.CompilerParams(
            dimension_semantics=("arbitrary", "arbitrary", "arbitrary"), vmem_limit_bytes=VMEM_LIMIT),
        name="attn_core",
    )(fq, fk, fv, fg, mq, mk, mv, mg)


def _attn_out_kernel(y_ref, x_ref, w_ref, g_ref, b_ref, o_ref):
    f = _dot(y_ref[0, 0], w_ref[0]) + _dot(y_ref[0, 1], w_ref[1])
    z = DEEPNORM_ALPHA * x_ref[0] + f
    o_ref[0] = _layer_norm(z, g_ref[...], b_ref[...])


def _attn_out_proj(y, x, w_out, ln_g, ln_b, *, tm=512):
    B, S, D = x.shape
    full = lambda shape: pl.BlockSpec(shape, lambda b, s: (0,) * len(shape))
    tile = pl.BlockSpec((1, tm, D), lambda b, s: (b, s, 0))
    return pl.pallas_call(
        _attn_out_kernel,
        grid=(B, S // tm),
        in_specs=[pl.BlockSpec((1, 2, tm, HEAD_WIDTH), lambda b, s: (b, 0, s, 0)), tile,
                  full(w_out.shape), full(ln_g.shape), full(ln_b.shape)],
        out_specs=tile,
        out_shape=jax.ShapeDtypeStruct((B, S, D), F32),
        compiler_params=pltpu.CompilerParams(
            dimension_semantics=("arbitrary", "arbitrary"), vmem_limit_bytes=VMEM_LIMIT),
        name="attn_out_proj",
    )(y, x, w_out, ln_g, ln_b)


def _pool_kernel(x_ref, win_ref, wgrp_ref, scale_ref, wout_ref, g_ref, b_ref, o_ref, ubuf_ref, *, tm):
    s = pl.program_id(1)
    x = x_ref[0]
    h = _dot(x.astype(BF16), win_ref[...])

    @pl.when(s == 0)
    def _():
        ubuf_ref[0:POOL_HALO, :] = jnp.zeros((POOL_HALO, POOL_WIDTH), F32)

    @pl.when(s > 0)
    def _():
        ubuf_ref[0:POOL_HALO, :] = ubuf_ref[tm:tm + POOL_HALO, :]

    ubuf_ref[POOL_HALO:, :] = h[:, :POOL_WIDTH]

    t = s * tm + lax.broadcasted_iota(jnp.int32, (tm, LANES), 0)
    f = jnp.zeros((tm, D_MODEL), F32)
    for g, w in enumerate(POOL_WINDOWS):
        cols = slice(g * POOL_GROUP, (g + 1) * POOL_GROUP)
        win = ubuf_ref[:, cols]
        span = 1
        while span < w:
            win = win + pltpu.roll(win, span, 0)
            span *= 2
        inv = 1.0 / jnp.minimum(t + 1, w).astype(F32)
        inv = jnp.concatenate([inv] * (POOL_GROUP // LANES), axis=-1)
        pooled = win[POOL_HALO:, :] * inv - h[:, cols]
        y = _dot(pooled.astype(BF16), wgrp_ref[g]) * scale_ref[:, cols]
        gate = h[:, POOL_WIDTH + g * POOL_GROUP:POOL_WIDTH + (g + 1) * POOL_GROUP]
        f = f + _dot((y * _silu(gate)).astype(BF16), wout_ref[cols, :])
    z = DEEPNORM_ALPHA * x + f
    o_ref[0] = _layer_norm(z, g_ref[...], b_ref[...])


def _pool_layer(x, w_in, w_grp, scale, w_out, ln_g, ln_b, *, tm=256):
    B, S, D = x.shape
    full = lambda shape: pl.BlockSpec(shape, lambda b, s: (0,) * len(shape))
    tile = pl.BlockSpec((1, tm, D), lambda b, s: (b, s, 0))
    return pl.pallas_call(
        functools.partial(_pool_kernel, tm=tm),
        grid=(B, S // tm),
        in_specs=[tile, full(w_in.shape), full(w_grp.shape), full(scale.shape), full(w_out.shape),
                  full(ln_g.shape), full(ln_b.shape)],
        out_specs=tile,
        out_shape=jax.ShapeDtypeStruct((B, S, D), F32),
        scratch_shapes=[pltpu.VMEM((POOL_HALO + tm, POOL_WIDTH), F32)],
        compiler_params=pltpu.CompilerParams(
            dimension_semantics=("arbitrary", "arbitrary"), vmem_limit_bytes=VMEM_LIMIT),
        name="pool_layer",
    )(x, w_in, w_grp, scale, w_out, ln_g, ln_b)


def _attn_layer(x, w_in, b_f, w_out, ln_g, ln_b):
    W = HEAD_WIDTH
    n_ff = N_HEADS
    w_main = jnp.concatenate([w_in[:, :W] * Q_SCALE, w_in[:, W:4 * W],
                              w_in[:, 4 * W + n_ff:5 * W + n_ff] * Q_SCALE, w_in[:, 5 * W + n_ff:]],
                             axis=1).astype(BF16)
    w_ff = jnp.pad(w_in[:, 4 * W:4 * W + n_ff], ((0, 0), (0, LANES - n_ff))).astype(BF16)
    b_pad = jnp.pad(b_f, (0, LANES - n_ff)).reshape(1, LANES)
    y = _attn_core(*_attn_in_proj(x, w_main, w_ff, b_pad))
    return _attn_out_proj(y, x, w_out.reshape(2, W, D_MODEL).astype(BF16),
                          ln_g.reshape(1, -1), ln_b.reshape(1, -1))


def kernel(x, attn_w_in, attn_b_f, attn_w_out, pool_w_in, pool_w_grp, pool_scale, pool_w_out, ln_g, ln_b):
    for layer in range(DEPTH):
        j = layer // 2
        if layer % 2 == 0:
            x = _attn_layer(x, attn_w_in[j], attn_b_f[j], attn_w_out[j], ln_g[layer], ln_b[layer])
        else:
            x = _pool_layer(x, pool_w_in[j].astype(BF16), pool_w_grp[j].astype(BF16),
                            pool_scale[j].reshape(1, -1), pool_w_out[j].astype(BF16),
                            ln_g[layer].reshape(1, -1), ln_b[layer].reshape(1, -1))
    return x
```

```python
import functools
import math

import jax
import jax.numpy as jnp
import numpy as np
from jax import lax
from jax.experimental import pallas as pl
from jax.experimental.pallas import tpu as pltpu

D_MODEL = 1024
DEPTH = 4
HEAD_DIM = 64
N_HEADS = 8
HEAD_WIDTH = N_HEADS * HEAD_DIM
N_PAIRS = N_HEADS // 2
MOBA_BLOCK = 256
MOBA_TOPK = 3
POOL_WIDTH = 2048
POOL_WINDOWS = (2, 4, 8, 16)
POOL_GROUP = 512
POOL_HALO = 16
DEEPNORM_ALPHA = (2 * DEPTH) ** 0.25
LN_EPS = 1e-5
LOG2E = math.log2(math.e)
Q_SCALE = HEAD_DIM ** -0.5 * LOG2E

LANES = 128
SUBLANES = 8
MASK_BIAS = -30000.0
VMEM_LIMIT = 56 * 1024 * 1024

BF16 = jnp.bfloat16
F32 = jnp.float32


def _dot(a, b):
    return jnp.dot(a, b, preferred_element_type=F32)


def _dot_nt(a, b):
    return lax.dot_general(a, b, (((1,), (1,)), ((), ())), preferred_element_type=F32)


def _split3(v):
    hi = v.astype(BF16)
    r1 = v - hi.astype(F32)
    mid = r1.astype(BF16)
    lo = (r1 - mid.astype(F32)).astype(BF16)
    return hi, mid, lo


def _layer_norm(z, g, b):
    mu = jnp.mean(z, axis=-1, keepdims=True)
    zc = z - mu
    var = jnp.mean(zc * zc, axis=-1, keepdims=True)
    return zc * lax.rsqrt(var + LN_EPS) * g + b


def _silu(v):
    return v * jax.nn.sigmoid(v)


def _attn_in_kernel(x_ref, w_ref, wff_ref, bf_ref, place_ref, const_ref,
                    fq_ref, fk_ref, fv_ref, fg_ref, mq_ref, mk_ref, mv_ref, mg_ref,
                    carry_ref, *, tm):
    s = pl.program_id(1)
    xb = x_ref[0].astype(BF16)
    h = _dot(xb, w_ref[...])
    ff = _dot(xb, wff_ref[...]) + bf_ref[...]
    log_f = jax.nn.log_sigmoid(ff)

    @pl.when(s == 0)
    def _():
        carry_ref[...] = jnp.zeros_like(carry_ref)

    row = lax.broadcasted_iota(jnp.int32, (tm, tm), 0)
    col = lax.broadcasted_iota(jnp.int32, (tm, tm), 1)
    tri = jnp.where(col <= row, 1.0, 0.0).astype(BF16)
    hi, mid, lo = _split3(log_f)
    c = (_dot(tri, hi) + _dot(tri, mid)) + _dot(tri, lo) + carry_ref[0:1, :]
    carry_ref[0:1, :] = c[tm - 1:tm, :]

    c_hi, c_mid, c_lo = _split3(c * LOG2E)
    ext = _dot(jnp.concatenate([c_hi, c_mid, c_lo], axis=-1), place_ref[...]) + const_ref[...]

    blk = (s * tm + lax.broadcasted_iota(jnp.int32, (tm, LANES), 0)) // MOBA_BLOCK
    onehot = jnp.where(lax.broadcasted_iota(jnp.int32, (tm, LANES), 1) == blk, 1.0, 0.0).astype(BF16)

    W = HEAD_WIDTH
    head_rows = lax.broadcasted_iota(jnp.int32, (LANES, tm), 0) < HEAD_DIM
    for p in range(N_PAIRS):
        sl = slice(p * LANES, (p + 1) * LANES)
        lo_lanes = slice(2 * p * LANES, (2 * p + 1) * LANES)
        hi_lanes = slice((2 * p + 1) * LANES, (2 * p + 2) * LANES)
        fq_ref[0, :, lo_lanes] = h[:, sl].astype(BF16)
        fq_ref[0, :, hi_lanes] = ext[:, sl].astype(BF16)
        fk_ref[0, :, lo_lanes] = h[:, W + p * LANES:W + (p + 1) * LANES].astype(BF16)
        fk_ref[0, :, hi_lanes] = ext[:, W + p * LANES:W + (p + 1) * LANES].astype(BF16)
        mk_ref[0, :, lo_lanes] = h[:, 5 * W + p * LANES:5 * W + (p + 1) * LANES].astype(BF16)
        mk_ref[0, :, hi_lanes] = onehot
        for v_ref, base in ((fv_ref, 2 * W), (mv_ref, 6 * W)):
            vt = h[:, base + p * LANES:base + (p + 1) * LANES].T
            v_ref[0, lo_lanes, :] = jnp.where(head_rows, vt, 1.0).astype(BF16)
            v_ref[0, hi_lanes, :] = jnp.where(head_rows, 1.0, vt).astype(BF16)
    fg_ref[0] = _silu(h[:, 3 * W:4 * W]).astype(BF16)
    mq_ref[0] = h[:, 4 * W:5 * W].astype(BF16)
    mg_ref[0] = _silu(h[:, 7 * W:8 * W]).astype(BF16)


def _decay_placement():
    place = np.zeros((3 * LANES, 2 * HEAD_WIDTH), np.float32)
    const = np.zeros((1, 2 * HEAD_WIDTH), np.float32)
    for h in range(N_HEADS):
        p, a = divmod(h, 2)
        base_q = p * LANES + a * HEAD_DIM
        base_k = HEAD_WIDTH + p * LANES + a * HEAD_DIM
        for r in range(3):
            place[r * LANES + h, base_q + r] = 1.0
            const[0, base_q + 3 + r] = 1.0
            const[0, base_k + r] = 1.0
            place[r * LANES + h, base_k + 3 + r] = -1.0
    return jnp.asarray(place, BF16), jnp.asarray(const, F32)


def _attn_in_proj(x, w_main, w_ff, b_f, *, tm=256):
    B, S, D = x.shape
    place, const = _decay_placement()
    W = HEAD_WIDTH
    full = lambda shape: pl.BlockSpec(shape, lambda b, s: (0,) * len(shape))
    tile = lambda width: pl.BlockSpec((1, tm, width), lambda b, s: (b, s, 0))
    row_major = lambda w: (tile(w), jax.ShapeDtypeStruct((B, S, w), BF16))
    transposed = (pl.BlockSpec((1, 2 * W, tm), lambda b, s: (b, 0, s)), jax.ShapeDtypeStruct((B, 2 * W, S), BF16))
    outs = (row_major(2 * W), row_major(2 * W), transposed, row_major(W),
            row_major(W), row_major(2 * W), transposed, row_major(W))
    return pl.pallas_call(
        functools.partial(_attn_in_kernel, tm=tm),
        grid=(B, S // tm),
        in_specs=[tile(D), full(w_main.shape), full(w_ff.shape), full(b_f.shape),
                  full(place.shape), full(const.shape)],
        out_specs=[o[0] for o in outs],
        out_shape=[o[1] for o in outs],
        scratch_shapes=[pltpu.VMEM((8, LANES), F32)],
        compiler_params=pltpu.CompilerParams(
            dimension_semantics=("arbitrary", "arbitrary"), vmem_limit_bytes=VMEM_LIMIT),
        name="attn_in_proj",
    )(x, w_main, w_ff, b_f, place, const)


def _head_row_mask(shape, a):
    row = lax.broadcasted_iota(jnp.int32, shape, 0)
    return (row // HEAD_DIM) % 2 == a


def _moba_bias_t(qa_t, kmean, qi, nb):
    tq = qa_t.shape[1]
    gate = _dot(kmean, qa_t)[0:SUBLANES, :]
    blk = lax.broadcasted_iota(jnp.int32, gate.shape, 0)
    gate = jnp.where(blk < qi, gate, -jnp.inf)
    rank = jnp.zeros(gate.shape, jnp.int32)
    for d in range(1, nb):
        other = pltpu.roll(gate, d, 0)
        rank = rank + jnp.where(blk >= d, jnp.where(other >= gate, 1, 0), jnp.where(other > gate, 1, 0))
    keep = ((rank < MOBA_TOPK) & (gate > -jnp.inf)) | (blk == qi)
    bias_t = jnp.where(keep, 0.0, MASK_BIAS)
    return jnp.concatenate([bias_t, jnp.zeros((LANES - SUBLANES, tq), F32)], axis=0).astype(BF16)


def _attn_core_kernel(fq_ref, fk_ref, fv_ref, fg_ref, mq_ref, mk_ref, mv_ref, mg_ref, y_ref,
                      kmean_ref, qt_ref, s_ref, acc_ref, *, seq):
    tq = MOBA_BLOCK
    nb = seq // tq
    assert nb == SUBLANES
    qi = pl.program_id(1)

    @pl.when(qi == 0)
    def _():
        r = lax.broadcasted_iota(jnp.int32, (LANES, seq), 0)
        t = lax.broadcasted_iota(jnp.int32, (LANES, seq), 1)
        avg = jnp.where(t // tq == r, 1.0 / tq, 0.0).astype(BF16)
        for p in range(N_PAIRS):
            kmean_ref[p] = _dot(avg, mk_ref[0, :, 2 * p * LANES:(2 * p + 1) * LANES])

    chains = []
    for p in range(N_PAIRS):
        fq_t = fq_ref[0, :, 2 * p * LANES:(2 * p + 2) * LANES].astype(F32).T.astype(BF16)
        for a in range(2):
            qt_ref[len(chains)] = jnp.where(_head_row_mask(fq_t.shape, a), fq_t, jnp.zeros_like(fq_t))
            chains.append((fk_ref, 2 * p * LANES, fv_ref, (2 * p + a) * LANES))
    for p in range(N_PAIRS):
        mq_t = mq_ref[0, :, p * LANES:(p + 1) * LANES].astype(F32).T.astype(BF16)
        kmean = kmean_ref[p].astype(BF16)
        for a in range(2):
            qa_t = jnp.where(_head_row_mask(mq_t.shape, a), mq_t, jnp.zeros_like(mq_t))
            qt_ref[len(chains)] = jnp.concatenate([qa_t, _moba_bias_t(qa_t, kmean, qi, nb)], axis=0)
            chains.append((mk_ref, 2 * p * LANES, mv_ref, (2 * p + a) * LANES))
    n = len(chains)

    def tile_step(start, ms, mask=None, first=False):
        tile_max = []
        for c, (k_ref, k_off, _, _) in enumerate(chains):
            s_t = _dot(k_ref[0, pl.ds(start, tq), k_off:k_off + 2 * LANES], qt_ref[c])
            if mask is not None:
                s_t = jnp.where(mask, s_t, -jnp.inf)
            s_ref[c] = s_t
            tile_max.append(jnp.max(s_t, axis=0, keepdims=True))
        new_ms = []
        for c, (m, (_, _, v_ref, v_row)) in enumerate(zip(ms, chains)):
            m_new = jnp.maximum(m, tile_max[c])
            new_ms.append(m_new)
            p_t = jnp.exp2(s_ref[c] - m_new).astype(BF16)
            pv = _dot(v_ref[0, v_row:v_row + LANES, pl.ds(start, tq)], p_t)
            acc_ref[c] = pv if first else jnp.exp2(m - m_new) * acc_ref[c] + pv
        return new_ms

    key = lax.broadcasted_iota(jnp.int32, (tq, tq), 0)
    qry = lax.broadcasted_iota(jnp.int32, (tq, tq), 1)
    ms = tile_step(pl.multiple_of(qi * tq, tq), [jnp.full((1, tq), -jnp.inf, F32)] * n,
                   mask=key <= qry, first=True)
    lax.fori_loop(0, qi, lambda j, ms: tile_step(pl.multiple_of(j * tq, tq), ms), ms)

    head0_rows = lax.broadcasted_iota(jnp.int32, (LANES, tq), 0) < HEAD_DIM
    for kind, g_ref in enumerate((fg_ref, mg_ref)):
        for p in range(N_PAIRS):
            c0 = kind * N_HEADS + 2 * p
            o = [acc_ref[c] / pltpu.roll(acc_ref[c], HEAD_DIM, 0) for c in (c0, c0 + 1)]
            y = jnp.where(head0_rows, o[0], o[1]).T
            lanes = slice(p * LANES, (p + 1) * LANES)
            y_ref[0, kind, :, lanes] = (y * g_ref[0, :, lanes].astype(F32)).astype(BF16)


def _attn_core(fq, fk, fv, fg, mq, mk, mv, mg):
    B, S, _ = fq.shape
    tq = MOBA_BLOCK
    n_chains = 2 * N_HEADS
    q_tile = lambda width: pl.BlockSpec((1, tq, width), lambda b, i: (b, i, 0))
    k_full = pl.BlockSpec((1, S, 2 * HEAD_WIDTH), lambda b, i: (b, 0, 0))
    v_full = pl.BlockSpec((1, 2 * HEAD_WIDTH, S), lambda b, i: (b, 0, 0))
    return pl.pallas_call(
        functools.partial(_attn_core_kernel, seq=S),
        grid=(B, S // tq),
        in_specs=[q_tile(2 * HEAD_WIDTH), k_full, v_full, q_tile(HEAD_WIDTH),
                  q_tile(HEAD_WIDTH), k_full, v_full, q_tile(HEAD_WIDTH)],
        out_specs=pl.BlockSpec((1, 2, tq, HEAD_WIDTH), lambda b, i: (b, 0, i, 0)),
        out_shape=jax.ShapeDtypeStruct((B, 2, S, HEAD_WIDTH), BF16),
        scratch_shapes=[pltpu.VMEM((N_PAIRS, LANES, LANES), F32),
                        pltpu.VMEM((n_chains, 2 * LANES, tq), BF16),
                        pltpu.VMEM((n_chains, tq, tq), F32),
                        pltpu.VMEM((n_chains, LANES, tq), F32)],
        compiler_params=pltpu.CompilerParams(
            dimension_semantics=("arbitrary", "arbitrary"), vmem_limit_bytes=VMEM_LIMIT),
        name="attn_core",
    )(fq, fk, fv, fg, mq, mk, mv, mg)


def _attn_out_kernel(y_ref, x_ref, w_ref, g_ref, b_ref, o_ref):
    f = _dot(y_ref[0, 0], w_ref[0]) + _dot(y_ref[0, 1], w_ref[1])
    z = DEEPNORM_ALPHA * x_ref[0] + f
    o_ref[0] = _layer_norm(z, g_ref[...], b_ref[...])


def _attn_out_proj(y, x, w_out, ln_g, ln_b, *, tm=512):
    B, S, D = x.shape
    full = lambda shape: pl.BlockSpec(shape, lambda b, s: (0,) * len(shape))
    tile = pl.BlockSpec((1, tm, D), lambda b, s: (b, s, 0))
    return pl.pallas_call(
        _attn_out_kernel,
        grid=(B, S // tm),
        in_specs=[pl.BlockSpec((1, 2, tm, HEAD_WIDTH), lambda b, s: (b, 0, s, 0)), tile,
                  full(w_out.shape), full(ln_g.shape), full(ln_b.shape)],
        out_specs=tile,
        out_shape=jax.ShapeDtypeStruct((B, S, D), F32),
        compiler_params=pltpu.CompilerParams(
            dimension_semantics=("arbitrary", "arbitrary"), vmem_limit_bytes=VMEM_LIMIT),
        name="attn_out_proj",
    )(y, x, w_out, ln_g, ln_b)


def _pool_kernel(x_ref, win_ref, wgrp_ref, scale_ref, wout_ref, g_ref, b_ref, o_ref, ubuf_ref, *, tm):
    s = pl.program_id(1)
    x = x_ref[0]
    xb = x.astype(BF16)

    @pl.when(s == 0)
    def _():
        ubuf_ref[0:POOL_HALO, :] = jnp.zeros((POOL_HALO, POOL_WIDTH), F32)

    @pl.when(s > 0)
    def _():
        ubuf_ref[0:POOL_HALO, :] = ubuf_ref[tm:tm + POOL_HALO, :]

    t = s * tm + lax.broadcasted_iota(jnp.int32, (tm, LANES), 0)

    def project(g):
        cols = slice(g * POOL_GROUP, (g + 1) * POOL_GROUP)
        gcols = slice(POOL_WIDTH + g * POOL_GROUP, POOL_WIDTH + (g + 1) * POOL_GROUP)
        return _dot(xb, win_ref[:, cols]), _dot(xb, win_ref[:, gcols])

    def mix(g, u, gate):
        w = POOL_WINDOWS[g]
        cols = slice(g * POOL_GROUP, (g + 1) * POOL_GROUP)
        ubuf_ref[POOL_HALO:, cols] = u
        win = ubuf_ref[:, cols]
        span = 1
        while span < w:
            win = win + pltpu.roll(win, span, 0)
            span *= 2
        inv = 1.0 / jnp.minimum(t + 1, w).astype(F32)
        inv = jnp.concatenate([inv] * (POOL_GROUP // LANES), axis=-1)
        pooled = win[POOL_HALO:, :] * inv - u
        y = _dot(pooled.astype(BF16), wgrp_ref[g]) * scale_ref[:, cols]
        return _dot((y * _silu(gate)).astype(BF16), wout_ref[cols, :])

    n_groups = len(POOL_WINDOWS)
    ahead = project(0)
    f = jnp.zeros((tm, D_MODEL), F32)
    for g in range(n_groups):
        cur = ahead
        if g + 1 < n_groups:
            ahead = project(g + 1)
        f = f + mix(g, *cur)
    z = DEEPNORM_ALPHA * x + f
    o_ref[0] = _layer_norm(z, g_ref[...], b_ref[...])


def _pool_layer(x, w_in, w_grp, scale, w_out, ln_g, ln_b, *, tm=256):
    B, S, D = x.shape
    full = lambda shape: pl.BlockSpec(shape, lambda b, s: (0,) * len(shape))
    tile = pl.BlockSpec((1, tm, D), lambda b, s: (b, s, 0))
    return pl.pallas_call(
        functools.partial(_pool_kernel, tm=tm),
        grid=(B, S // tm),
        in_specs=[tile, full(w_in.shape), full(w_grp.shape), full(scale.shape), full(w_out.shape),
                  full(ln_g.shape), full(ln_b.shape)],
        out_specs=tile,
        out_shape=jax.ShapeDtypeStruct((B, S, D), F32),
        scratch_shapes=[pltpu.VMEM((POOL_HALO + tm, POOL_WIDTH), F32)],
        compiler_params=pltpu.CompilerParams(
            dimension_semantics=("arbitrary", "arbitrary"), vmem_limit_bytes=VMEM_LIMIT),
        name="pool_layer",
    )(x, w_in, w_grp, scale, w_out, ln_g, ln_b)


def _attn_layer(x, w_in, b_f, w_out, ln_g, ln_b):
    W = HEAD_WIDTH
    n_ff = N_HEADS
    w_main = jnp.concatenate([w_in[:, :W] * Q_SCALE, w_in[:, W:4 * W],
                              w_in[:, 4 * W + n_ff:5 * W + n_ff] * Q_SCALE, w_in[:, 5 * W + n_ff:]],
                             axis=1).astype(BF16)
    w_ff = jnp.pad(w_in[:, 4 * W:4 * W + n_ff], ((0, 0), (0, LANES - n_ff))).astype(BF16)
    b_pad = jnp.pad(b_f, (0, LANES - n_ff)).reshape(1, LANES)
    y = _attn_core(*_attn_in_proj(x, w_main, w_ff, b_pad))
    return _attn_out_proj(y, x, w_out.reshape(2, W, D_MODEL).astype(BF16),
                          ln_g.reshape(1, -1), ln_b.reshape(1, -1))


def kernel(x, attn_w_in, attn_b_f, attn_w_out, pool_w_in, pool_w_grp, pool_scale, pool_w_out, ln_g, ln_b):
    for layer in range(DEPTH):
        j = layer // 2
        if layer % 2 == 0:
            x = _attn_layer(x, attn_w_in[j], attn_b_f[j], attn_w_out[j], ln_g[layer], ln_b[layer])
        else:
            x = _pool_layer(x, pool_w_in[j].astype(BF16), pool_w_grp[j].astype(BF16),
                            pool_scale[j].reshape(1, -1), pool_w_out[j].astype(BF16),
                            ln_g[layer].reshape(1, -1), ln_b[layer].reshape(1, -1))
    return x
```

```python
import functools
import math

import jax
import jax.numpy as jnp
import numpy as np
from jax import lax
from jax.experimental import pallas as pl
from jax.experimental.pallas import tpu as pltpu

D_MODEL = 1024
DEPTH = 4
HEAD_DIM = 64
N_HEADS = 8
HEAD_WIDTH = N_HEADS * HEAD_DIM
N_PAIRS = N_HEADS // 2
MOBA_BLOCK = 256
MOBA_TOPK = 3
POOL_WIDTH = 2048
POOL_WINDOWS = (2, 4, 8, 16)
POOL_GROUP = 512
POOL_HALO = 16
DEEPNORM_ALPHA = (2 * DEPTH) ** 0.25
LN_EPS = 1e-5
LOG2E = math.log2(math.e)
Q_SCALE = HEAD_DIM ** -0.5 * LOG2E

LANES = 128
SUBLANES = 8
BF16_SUBLANES = 2 * SUBLANES
V_ROWS = HEAD_DIM + BF16_SUBLANES
MASK_BIAS = -30000.0
VMEM_LIMIT = 56 * 1024 * 1024

BF16 = jnp.bfloat16
F32 = jnp.float32


def _dot(a, b):
    return jnp.dot(a, b, preferred_element_type=F32)


def _split3(v):
    hi = v.astype(BF16)
    r1 = v - hi.astype(F32)
    mid = r1.astype(BF16)
    lo = (r1 - mid.astype(F32)).astype(BF16)
    return hi, mid, lo


def _layer_norm(z, g, b):
    mu = jnp.mean(z, axis=-1, keepdims=True)
    zc = z - mu
    var = jnp.mean(zc * zc, axis=-1, keepdims=True)
    return zc * lax.rsqrt(var + LN_EPS) * g + b


def _silu(v):
    return v * jax.nn.sigmoid(v)


def _pack3(v):
    lane = lax.broadcasted_iota(jnp.int32, v.shape, 1)
    hi, mid, lo = (part.astype(F32) for part in _split3(v))
    packed = jnp.where(lane < N_HEADS, hi,
                       jnp.where(lane < 2 * N_HEADS, pltpu.roll(mid, N_HEADS, 1),
                                 jnp.where(lane < 3 * N_HEADS, pltpu.roll(lo, 2 * N_HEADS, 1), 0.0)))
    return packed.astype(BF16)


def _attn_in_kernel(x_ref, w_ref, wff_ref, bf_ref, place_ref, const_ref,
                    fq_ref, fk_ref, fv_ref, fg_ref, mq_ref, mk_ref, mv_ref, mg_ref,
                    carry_ref, *, tm):
    s = pl.program_id(1)
    xb = x_ref[0].astype(BF16)
    W = HEAD_WIDTH

    def project(i):
        return _dot(xb, w_ref[:, i * W:(i + 1) * W])

    blk = (s * tm + lax.broadcasted_iota(jnp.int32, (tm, LANES), 0)) // MOBA_BLOCK
    onehot = jnp.where(lax.broadcasted_iota(jnp.int32, (tm, LANES), 1) == blk, 1.0, 0.0).astype(BF16)

    def pair_lanes(p):
        return slice(2 * p * LANES, (2 * p + 1) * LANES), slice((2 * p + 1) * LANES, (2 * p + 2) * LANES)

    def store_values(v_ref, h):
        ones = jnp.ones((V_ROWS - HEAD_DIM, tm), BF16)
        for p in range(N_PAIRS):
            vt = h[:, p * LANES:(p + 1) * LANES].T.astype(BF16)
            for a in range(2):
                r0 = (2 * p + a) * V_ROWS
                v_ref[0, r0:r0 + HEAD_DIM, :] = vt[a * HEAD_DIM:(a + 1) * HEAD_DIM, :]
                v_ref[0, r0 + HEAD_DIM:r0 + V_ROWS, :] = ones

    ff = _dot(xb, wff_ref[...]) + bf_ref[...]
    log_f_parts = _pack3(jax.nn.log_sigmoid(ff))
    store_values(fv_ref, project(2))
    fg_ref[0] = _silu(project(3)).astype(BF16)

    @pl.when(s == 0)
    def _():
        carry_ref[...] = jnp.zeros_like(carry_ref)

    row = lax.broadcasted_iota(jnp.int32, (tm, tm), 0)
    col = lax.broadcasted_iota(jnp.int32, (tm, tm), 1)
    tri = jnp.where(col <= row, 1.0, 0.0).astype(BF16)
    parts = _dot(tri, log_f_parts)
    c = (parts + pltpu.roll(parts, LANES - N_HEADS, 1)) + pltpu.roll(parts, LANES - 2 * N_HEADS, 1)
    c = c + carry_ref[0:1, :]
    carry_ref[0:1, :] = c[tm - 1:tm, :]
    c_parts = _pack3(c * LOG2E)
    h = project(4)
    for p in range(N_PAIRS):
        mq_ref[0, p * LANES:(p + 1) * LANES, :] = h[:, p * LANES:(p + 1) * LANES].T.astype(BF16)
    h = project(5)
    for p in range(N_PAIRS):
        lo_lanes, hi_lanes = pair_lanes(p)
        mk_ref[0, :, lo_lanes] = h[:, p * LANES:(p + 1) * LANES].astype(BF16)
        mk_ref[0, :, hi_lanes] = onehot
    ext = _dot(c_parts, place_ref[...]) + const_ref[...]

    store_values(mv_ref, project(6))
    mg_ref[0] = _silu(project(7)).astype(BF16)
    h = project(0)
    for p in range(N_PAIRS):
        lo_lanes, hi_lanes = pair_lanes(p)
        fq_ref[0, lo_lanes, :] = h[:, p * LANES:(p + 1) * LANES].T.astype(BF16)
        fq_ref[0, hi_lanes, :] = ext[:, p * LANES:(p + 1) * LANES].T.astype(BF16)
    h = project(1)
    for p in range(N_PAIRS):
        lo_lanes, hi_lanes = pair_lanes(p)
        fk_ref[0, :, lo_lanes] = h[:, p * LANES:(p + 1) * LANES].astype(BF16)
        fk_ref[0, :, hi_lanes] = ext[:, W + p * LANES:W + (p + 1) * LANES].astype(BF16)


def _decay_placement():
    place = np.zeros((LANES, 2 * HEAD_WIDTH), np.float32)
    const = np.zeros((1, 2 * HEAD_WIDTH), np.float32)
    for h in range(N_HEADS):
        p, a = divmod(h, 2)
        base_q = p * LANES + a * HEAD_DIM
        base_k = HEAD_WIDTH + p * LANES + a * HEAD_DIM
        for r in range(3):
            place[r * N_HEADS + h, base_q + r] = 1.0
            const[0, base_q + 3 + r] = 1.0
            const[0, base_k + r] = 1.0
            place[r * N_HEADS + h, base_k + 3 + r] = -1.0
    return jnp.asarray(place, BF16), jnp.asarray(const, F32)


def _attn_in_proj(x, w_main, w_ff, b_f, *, tm=256):
    B, S, D = x.shape
    place, const = _decay_placement()
    W = HEAD_WIDTH
    full = lambda shape: pl.BlockSpec(shape, lambda b, s: (0,) * len(shape))
    tile = lambda width: pl.BlockSpec((1, tm, width), lambda b, s: (b, s, 0))
    row_major = lambda w: (tile(w), jax.ShapeDtypeStruct((B, S, w), BF16))
    transposed = lambda rows: (pl.BlockSpec((1, rows, tm), lambda b, s: (b, 0, s)),
                               jax.ShapeDtypeStruct((B, rows, S), BF16))
    values = transposed(N_HEADS * V_ROWS)
    outs = (transposed(2 * W), row_major(2 * W), values, row_major(W),
            transposed(W), row_major(2 * W), values, row_major(W))
    return pl.pallas_call(
        functools.partial(_attn_in_kernel, tm=tm),
        grid=(B, S // tm),
        in_specs=[tile(D), full(w_main.shape), full(w_ff.shape), full(b_f.shape),
                  full(place.shape), full(const.shape)],
        out_specs=[o[0] for o in outs],
        out_shape=[o[1] for o in outs],
        scratch_shapes=[pltpu.VMEM((8, LANES), F32)],
        compiler_params=pltpu.CompilerParams(
            dimension_semantics=("arbitrary", "arbitrary"), vmem_limit_bytes=VMEM_LIMIT),
        name="attn_in_proj",
    )(x, w_main, w_ff, b_f, place, const)


def _head_row_mask(shape, a):
    row = lax.broadcasted_iota(jnp.int32, shape, 0)
    return (row // HEAD_DIM) % 2 == a


def _moba_bias_t(qa_t, kmean, qi, nb):
    tq = qa_t.shape[1]
    gate = _dot(kmean, qa_t)[0:SUBLANES, :]
    blk = lax.broadcasted_iota(jnp.int32, gate.shape, 0)
    gate = jnp.where(blk < qi, gate, -jnp.inf)
    rank = jnp.zeros(gate.shape, jnp.int32)
    for d in range(1, nb):
        other = pltpu.roll(gate, d, 0)
        rank = rank + jnp.where(blk >= d, jnp.where(other >= gate, 1, 0), jnp.where(other > gate, 1, 0))
    keep = ((rank < MOBA_TOPK) & (gate > -jnp.inf)) | (blk == qi)
    bias_t = jnp.where(keep, 0.0, MASK_BIAS)
    return jnp.concatenate([bias_t, jnp.zeros((LANES - SUBLANES, tq), F32)], axis=0).astype(BF16)


def _attn_core_kernel(fq_ref, fk_ref, fv_ref, fg_ref, mq_ref, mk_ref, mv_ref, mg_ref, y_ref,
                      kmean_ref, qt_ref, s_ref, acc_ref, *, seq):
    tq = MOBA_BLOCK
    nb = seq // tq
    assert nb == SUBLANES
    qi = pl.program_id(1)

    @pl.when(qi == 0)
    def _():
        r = lax.broadcasted_iota(jnp.int32, (LANES, seq), 0)
        t = lax.broadcasted_iota(jnp.int32, (LANES, seq), 1)
        avg = jnp.where(t // tq == r, 1.0 / tq, 0.0).astype(BF16)
        for p in range(N_PAIRS):
            kmean_ref[p] = _dot(avg, mk_ref[0, :, 2 * p * LANES:(2 * p + 1) * LANES])

    chains = []
    for p in range(N_PAIRS):
        fq_t = fq_ref[0, 2 * p * LANES:(2 * p + 2) * LANES, :]
        for a in range(2):
            qt_ref[len(chains)] = jnp.where(_head_row_mask(fq_t.shape, a), fq_t, jnp.zeros_like(fq_t))
            chains.append((fk_ref, 2 * p * LANES, fv_ref, (2 * p + a) * V_ROWS))
    for p in range(N_PAIRS):
        mq_t = mq_ref[0, p * LANES:(p + 1) * LANES, :]
        kmean = kmean_ref[p].astype(BF16)
        for a in range(2):
            qa_t = jnp.where(_head_row_mask(mq_t.shape, a), mq_t, jnp.zeros_like(mq_t))
            qt_ref[len(chains)] = jnp.concatenate([qa_t, _moba_bias_t(qa_t, kmean, qi, nb)], axis=0)
            chains.append((mk_ref, 2 * p * LANES, mv_ref, (2 * p + a) * V_ROWS))
    n = len(chains)

    def tile_step(start, ms, mask=None, first=False):
        tile_max = []
        for c, (k_ref, k_off, _, _) in enumerate(chains):
            s_t = _dot(k_ref[0, pl.ds(start, tq), k_off:k_off + 2 * LANES], qt_ref[c])
            if mask is not None:
                s_t = jnp.where(mask, s_t, -jnp.inf)
            s_ref[c] = s_t
            tile_max.append(jnp.max(s_t, axis=0, keepdims=True))
        new_ms = []
        for c, (m, (_, _, v_ref, v_row)) in enumerate(zip(ms, chains)):
            m_new = jnp.maximum(m, tile_max[c])
            new_ms.append(m_new)
            p_t = jnp.exp2(s_ref[c] - m_new).astype(BF16)
            pv = _dot(v_ref[0, v_row:v_row + V_ROWS, pl.ds(start, tq)], p_t)
            acc_ref[c] = pv if first else jnp.exp2(m - m_new) * acc_ref[c] + pv
        return new_ms

    key = lax.broadcasted_iota(jnp.int32, (tq, tq), 0)
    qry = lax.broadcasted_iota(jnp.int32, (tq, tq), 1)
    ms = tile_step(pl.multiple_of(qi * tq, tq), [jnp.full((1, tq), -jnp.inf, F32)] * n,
                   mask=key <= qry, first=True)
    lax.fori_loop(0, qi, lambda j, ms: tile_step(pl.multiple_of(j * tq, tq), ms), ms)

    for kind, g_ref in enumerate((fg_ref, mg_ref)):
        for p in range(N_PAIRS):
            o = []
            for c in (kind * N_HEADS + 2 * p, kind * N_HEADS + 2 * p + 1):
                o.append(acc_ref[c, 0:HEAD_DIM, :] / acc_ref[c, HEAD_DIM:HEAD_DIM + 1, :])
            y = jnp.concatenate(o, axis=0).T
            lanes = slice(p * LANES, (p + 1) * LANES)
            y_ref[0, kind, :, lanes] = (y * g_ref[0, :, lanes].astype(F32)).astype(BF16)


def _attn_core(fq, fk, fv, fg, mq, mk, mv, mg):
    B, S, _ = fk.shape
    tq = MOBA_BLOCK
    n_chains = 2 * N_HEADS
    q_tile = lambda width: pl.BlockSpec((1, tq, width), lambda b, i: (b, i, 0))
    qt_tile = lambda rows: pl.BlockSpec((1, rows, tq), lambda b, i: (b, 0, i))
    k_full = pl.BlockSpec((1, S, 2 * HEAD_WIDTH), lambda b, i: (b, 0, 0))
    v_full = pl.BlockSpec((1, N_HEADS * V_ROWS, S), lambda b, i: (b, 0, 0))
    return pl.pallas_call(
        functools.partial(_attn_core_kernel, seq=S),
        grid=(B, S // tq),
        in_specs=[qt_tile(2 * HEAD_WIDTH), k_full, v_full, q_tile(HEAD_WIDTH),
                  qt_tile(HEAD_WIDTH), k_full, v_full, q_tile(HEAD_WIDTH)],
        out_specs=pl.BlockSpec((1, 2, tq, HEAD_WIDTH), lambda b, i: (b, 0, i, 0)),
        out_shape=jax.ShapeDtypeStruct((B, 2, S, HEAD_WIDTH), BF16),
        scratch_shapes=[pltpu.VMEM((N_PAIRS, LANES, LANES), F32),
                        pltpu.VMEM((n_chains, 2 * LANES, tq), BF16),
                        pltpu.VMEM((n_chains, tq, tq), F32),
                        pltpu.VMEM((n_chains, V_ROWS, tq), F32)],
        compiler_params=pltpu.CompilerParams(
            dimension_semantics=("arbitrary", "arbitrary"), vmem_limit_bytes=VMEM_LIMIT),
        name="attn_core",
    )(fq, fk, fv, fg, mq, mk, mv, mg)


def _attn_out_kernel(y_ref, x_ref, w_ref, g_ref, b_ref, o_ref):
    tm = x_ref.shape[1]
    for r in range(0, tm, tm // 2):
        rows = slice(r, r + tm // 2)
        f = _dot(y_ref[0, 0, rows, :], w_ref[0]) + _dot(y_ref[0, 1, rows, :], w_ref[1])
        z = DEEPNORM_ALPHA * x_ref[0, rows, :] + f
        o_ref[0, rows, :] = _layer_norm(z, g_ref[...], b_ref[...])


def _attn_out_proj(y, x, w_out, ln_g, ln_b, *, tm=512):
    B, S, D = x.shape
    full = lambda shape: pl.BlockSpec(shape, lambda b, s: (0,) * len(shape))
    tile = pl.BlockSpec((1, tm, D), lambda b, s: (b, s, 0))
    return pl.pallas_call(
        _attn_out_kernel,
        grid=(B, S // tm),
        in_specs=[pl.BlockSpec((1, 2, tm, HEAD_WIDTH), lambda b, s: (b, 0, s, 0)), tile,
                  full(w_out.shape), full(ln_g.shape), full(ln_b.shape)],
        out_specs=tile,
        out_shape=jax.ShapeDtypeStruct((B, S, D), F32),
        compiler_params=pltpu.CompilerParams(
            dimension_semantics=("arbitrary", "arbitrary"), vmem_limit_bytes=VMEM_LIMIT),
        name="attn_out_proj",
    )(y, x, w_out, ln_g, ln_b)


def _pool_kernel(x_ref, win_ref, wgrp_ref, scale_ref, wout_ref, g_ref, b_ref, o_ref, ubuf_ref, *, tm):
    s = pl.program_id(1)
    x = x_ref[0]
    xb = x.astype(BF16)

    @pl.when(s == 0)
    def _():
        ubuf_ref[0:POOL_HALO, :] = jnp.zeros((POOL_HALO, POOL_WIDTH), F32)

    @pl.when(s > 0)
    def _():
        ubuf_ref[0:POOL_HALO, :] = ubuf_ref[tm:tm + POOL_HALO, :]

    t = s * tm + lax.broadcasted_iota(jnp.int32, (tm, LANES), 0)

    def project(g):
        cols = slice(g * POOL_GROUP, (g + 1) * POOL_GROUP)
        gcols = slice(POOL_WIDTH + g * POOL_GROUP, POOL_WIDTH + (g + 1) * POOL_GROUP)
        return _dot(xb, win_ref[:, cols]), _dot(xb, win_ref[:, gcols])

    def mix(g, u, gate):
        w = POOL_WINDOWS[g]
        cols = slice(g * POOL_GROUP, (g + 1) * POOL_GROUP)
        ubuf_ref[POOL_HALO:, cols] = u
        win = ubuf_ref[:, cols]
        span = 1
        while span < w:
            win = win + pltpu.roll(win, span, 0)
            span *= 2
        inv = 1.0 / jnp.minimum(t + 1, w).astype(F32)
        inv = jnp.concatenate([inv] * (POOL_GROUP // LANES), axis=-1)
        pooled = win[POOL_HALO:, :] * inv - u
        y = _dot(pooled.astype(BF16), wgrp_ref[g]) * scale_ref[:, cols]
        return _dot((y * _silu(gate)).astype(BF16), wout_ref[cols, :])

    n_groups = len(POOL_WINDOWS)
    ahead = project(0)
    f = jnp.zeros((tm, D_MODEL), F32)
    for g in range(n_groups):
        cur = ahead
        if g + 1 < n_groups:
            ahead = project(g + 1)
        f = f + mix(g, *cur)
    z = DEEPNORM_ALPHA * x + f
    o_ref[0] = _layer_norm(z, g_ref[...], b_ref[...])


def _pool_layer(x, w_in, w_grp, scale, w_out, ln_g, ln_b, *, tm=256):
    B, S, D = x.shape
    full = lambda shape: pl.BlockSpec(shape, lambda b, s: (0,) * len(shape))
    tile = pl.BlockSpec((1, tm, D), lambda b, s: (b, s, 0))
    return pl.pallas_call(
        functools.partial(_pool_kernel, tm=tm),
        grid=(B, S // tm),
        in_specs=[tile, full(w_in.shape), full(w_grp.shape), full(scale.shape), full(w_out.shape),
                  full(ln_g.shape), full(ln_b.shape)],
        out_specs=tile,
        out_shape=jax.ShapeDtypeStruct((B, S, D), F32),
        scratch_shapes=[pltpu.VMEM((POOL_HALO + tm, POOL_WIDTH), F32)],
        compiler_params=pltpu.CompilerParams(
            dimension_semantics=("arbitrary", "arbitrary"), vmem_limit_bytes=VMEM_LIMIT),
        name="pool_layer",
    )(x, w_in, w_grp, scale, w_out, ln_g, ln_b)


def _attn_layer(x, w_in, b_f, w_out, ln_g, ln_b):
    W = HEAD_WIDTH
    n_ff = N_HEADS
    w_main = jnp.concatenate([w_in[:, :W] * Q_SCALE, w_in[:, W:4 * W],
                              w_in[:, 4 * W + n_ff:5 * W + n_ff] * Q_SCALE, w_in[:, 5 * W + n_ff:]],
                             axis=1).astype(BF16)
    w_ff = jnp.pad(w_in[:, 4 * W:4 * W + n_ff], ((0, 0), (0, LANES - n_ff))).astype(BF16)
    b_pad = jnp.pad(b_f, (0, LANES - n_ff)).reshape(1, LANES)
    y = _attn_core(*_attn_in_proj(x, w_main, w_ff, b_pad))
    return _attn_out_proj(y, x, w_out.reshape(2, W, D_MODEL).astype(BF16),
                          ln_g.reshape(1, -1), ln_b.reshape(1, -1))


def kernel(x, attn_w_in, attn_b_f, attn_w_out, pool_w_in, pool_w_grp, pool_scale, pool_w_out, ln_g, ln_b):
    for layer in range(DEPTH):
        j = layer // 2
        if layer % 2 == 0:
            x = _attn_layer(x, attn_w_in[j], attn_b_f[j], attn_w_out[j], ln_g[layer], ln_b[layer])
        else:
            x = _pool_layer(x, pool_w_in[j].astype(BF16), pool_w_grp[j].astype(BF16),
                            pool_scale[j].reshape(1, -1), pool_w_out[j].astype(BF16),
                            ln_g[layer].reshape(1, -1), ln_b[layer].reshape(1, -1))
    return x
```

```python
import functools
import math

import jax
import jax.numpy as jnp
import numpy as np
from jax import lax
from jax.experimental import pallas as pl
from jax.experimental.pallas import tpu as pltpu

D_MODEL = 1024
DEPTH = 4
HEAD_DIM = 64
N_HEADS = 8
HEAD_WIDTH = N_HEADS * HEAD_DIM
N_PAIRS = N_HEADS // 2
MOBA_BLOCK = 256
MOBA_TOPK = 3
POOL_WIDTH = 2048
POOL_WINDOWS = (2, 4, 8, 16)
POOL_GROUP = 512
POOL_HALO = 16
DEEPNORM_ALPHA = (2 * DEPTH) ** 0.25
LN_EPS = 1e-5
LOG2E = math.log2(math.e)
Q_SCALE = HEAD_DIM ** -0.5 * LOG2E

LANES = 128
SUBLANES = 8
BF16_SUBLANES = 2 * SUBLANES
V_ROWS = HEAD_DIM + BF16_SUBLANES
MASK_BIAS = -30000.0
VMEM_LIMIT = 56 * 1024 * 1024

BF16 = jnp.bfloat16
F32 = jnp.float32


def _dot(a, b):
    return jnp.dot(a, b, preferred_element_type=F32)


def _split3(v):
    hi = v.astype(BF16)
    r1 = v - hi.astype(F32)
    mid = r1.astype(BF16)
    lo = (r1 - mid.astype(F32)).astype(BF16)
    return hi, mid, lo


def _layer_norm(z, g, b):
    mu = jnp.mean(z, axis=-1, keepdims=True)
    zc = z - mu
    var = jnp.mean(zc * zc, axis=-1, keepdims=True)
    return zc * lax.rsqrt(var + LN_EPS) * g + b


def _silu(v):
    return v * jax.nn.sigmoid(v)


def _pack3(v):
    lane = lax.broadcasted_iota(jnp.int32, v.shape, 1)
    hi, mid, lo = (part.astype(F32) for part in _split3(v))
    packed = jnp.where(lane < N_HEADS, hi,
                       jnp.where(lane < 2 * N_HEADS, pltpu.roll(mid, N_HEADS, 1),
                                 jnp.where(lane < 3 * N_HEADS, pltpu.roll(lo, 2 * N_HEADS, 1), 0.0)))
    return packed.astype(BF16)


def _attn_in_kernel(x_ref, w_ref, wff_ref, bf_ref, place_ref, const_ref,
                    fq_ref, fk_ref, fv_ref, fg_ref, mq_ref, mk_ref, mv_ref, mg_ref,
                    carry_ref, *, tm):
    s = pl.program_id(1)
    xb = x_ref[0].astype(BF16)
    W = HEAD_WIDTH

    def project(i):
        return _dot(xb, w_ref[:, i * W:(i + 1) * W])

    blk = (s * tm + lax.broadcasted_iota(jnp.int32, (tm, LANES), 0)) // MOBA_BLOCK
    onehot = jnp.where(lax.broadcasted_iota(jnp.int32, (tm, LANES), 1) == blk, 1.0, 0.0).astype(BF16)

    def pair_lanes(p):
        return slice(2 * p * LANES, (2 * p + 1) * LANES), slice((2 * p + 1) * LANES, (2 * p + 2) * LANES)

    def store_values(v_ref, h):
        ones = jnp.ones((V_ROWS - HEAD_DIM, tm), BF16)
        for p in range(N_PAIRS):
            vt = h[:, p * LANES:(p + 1) * LANES].T.astype(BF16)
            for a in range(2):
                r0 = (2 * p + a) * V_ROWS
                v_ref[0, r0:r0 + HEAD_DIM, :] = vt[a * HEAD_DIM:(a + 1) * HEAD_DIM, :]
                v_ref[0, r0 + HEAD_DIM:r0 + V_ROWS, :] = ones

    ff = _dot(xb, wff_ref[...]) + bf_ref[...]
    log_f_parts = _pack3(jax.nn.log_sigmoid(ff))
    store_values(fv_ref, project(2))
    fg_ref[0] = _silu(project(3)).astype(BF16)

    @pl.when(s == 0)
    def _():
        carry_ref[...] = jnp.zeros_like(carry_ref)

    row = lax.broadcasted_iota(jnp.int32, (tm, tm), 0)
    col = lax.broadcasted_iota(jnp.int32, (tm, tm), 1)
    tri = jnp.where(col <= row, 1.0, 0.0).astype(BF16)
    parts = _dot(tri, log_f_parts)
    c = (parts + pltpu.roll(parts, LANES - N_HEADS, 1)) + pltpu.roll(parts, LANES - 2 * N_HEADS, 1)
    c = c + carry_ref[0:1, :]
    carry_ref[0:1, :] = c[tm - 1:tm, :]
    c_parts = _pack3(c * LOG2E)
    h = project(4)
    for p in range(N_PAIRS):
        mq_ref[0, p * LANES:(p + 1) * LANES, :] = h[:, p * LANES:(p + 1) * LANES].T.astype(BF16)
    h = project(5)
    for p in range(N_PAIRS):
        lo_lanes, hi_lanes = pair_lanes(p)
        mk_ref[0, :, lo_lanes] = h[:, p * LANES:(p + 1) * LANES].astype(BF16)
        mk_ref[0, :, hi_lanes] = onehot
    ext = _dot(c_parts, place_ref[...]) + const_ref[...]

    store_values(mv_ref, project(6))
    mg_ref[0] = _silu(project(7)).astype(BF16)
    h = project(0)
    for p in range(N_PAIRS):
        lo_lanes, hi_lanes = pair_lanes(p)
        fq_ref[0, lo_lanes, :] = h[:, p * LANES:(p + 1) * LANES].T.astype(BF16)
        fq_ref[0, hi_lanes, :] = ext[:, p * LANES:(p + 1) * LANES].T.astype(BF16)
    h = project(1)
    for p in range(N_PAIRS):
        lo_lanes, hi_lanes = pair_lanes(p)
        fk_ref[0, :, lo_lanes] = h[:, p * LANES:(p + 1) * LANES].astype(BF16)
        fk_ref[0, :, hi_lanes] = ext[:, W + p * LANES:W + (p + 1) * LANES].astype(BF16)


def _decay_placement():
    place = np.zeros((LANES, 2 * HEAD_WIDTH), np.float32)
    const = np.zeros((1, 2 * HEAD_WIDTH), np.float32)
    for h in range(N_HEADS):
        p, a = divmod(h, 2)
        base_q = p * LANES + a * HEAD_DIM
        base_k = HEAD_WIDTH + p * LANES + a * HEAD_DIM
        for r in range(3):
            place[r * N_HEADS + h, base_q + r] = 1.0
            const[0, base_q + 3 + r] = 1.0
            const[0, base_k + r] = 1.0
            place[r * N_HEADS + h, base_k + 3 + r] = -1.0
    return jnp.asarray(place, BF16), jnp.asarray(const, F32)


def _attn_in_proj(x, w_main, w_ff, b_f, *, tm=256):
    B, S, D = x.shape
    place, const = _decay_placement()
    W = HEAD_WIDTH
    full = lambda shape: pl.BlockSpec(shape, lambda b, s: (0,) * len(shape))
    tile = lambda width: pl.BlockSpec((1, tm, width), lambda b, s: (b, s, 0))
    row_major = lambda w: (tile(w), jax.ShapeDtypeStruct((B, S, w), BF16))
    transposed = lambda rows: (pl.BlockSpec((1, rows, tm), lambda b, s: (b, 0, s)),
                               jax.ShapeDtypeStruct((B, rows, S), BF16))
    values = transposed(N_HEADS * V_ROWS)
    outs = (transposed(2 * W), row_major(2 * W), values, row_major(W),
            transposed(W), row_major(2 * W), values, row_major(W))
    return pl.pallas_call(
        functools.partial(_attn_in_kernel, tm=tm),
        grid=(B, S // tm),
        in_specs=[tile(D), full(w_main.shape), full(w_ff.shape), full(b_f.shape),
                  full(place.shape), full(const.shape)],
        out_specs=[o[0] for o in outs],
        out_shape=[o[1] for o in outs],
        scratch_shapes=[pltpu.VMEM((8, LANES), F32)],
        compiler_params=pltpu.CompilerParams(
            dimension_semantics=("arbitrary", "arbitrary"), vmem_limit_bytes=VMEM_LIMIT),
        name="attn_in_proj",
    )(x, w_main, w_ff, b_f, place, const)


def _head_row_mask(shape, a):
    row = lax.broadcasted_iota(jnp.int32, shape, 0)
    return (row // HEAD_DIM) % 2 == a


def _moba_bias_t(qa_t, kmean, qi, nb):
    tq = qa_t.shape[1]
    gate = _dot(kmean, qa_t)[0:SUBLANES, :]
    blk = lax.broadcasted_iota(jnp.int32, gate.shape, 0)
    gate = jnp.where(blk < qi, gate, -jnp.inf)
    rank = jnp.zeros(gate.shape, jnp.int32)
    for d in range(1, nb):
        other = pltpu.roll(gate, d, 0)
        rank = rank + jnp.where(blk >= d, jnp.where(other >= gate, 1, 0), jnp.where(other > gate, 1, 0))
    keep = ((rank < MOBA_TOPK) & (gate > -jnp.inf)) | (blk == qi)
    bias_t = jnp.where(keep, 0.0, MASK_BIAS)
    return jnp.concatenate([bias_t, jnp.zeros((LANES - SUBLANES, tq), F32)], axis=0).astype(BF16)


def _attn_core_kernel(fq_ref, fk_ref, fv_ref, fg_ref, mq_ref, mk_ref, mv_ref, mg_ref, y_ref,
                      kmean_ref, qt_ref, s_ref, acc_ref, *, seq):
    tq = MOBA_BLOCK
    nb = seq // tq
    assert nb == SUBLANES
    qi = pl.program_id(1)

    @pl.when(qi == 0)
    def _():
        r = lax.broadcasted_iota(jnp.int32, (LANES, seq), 0)
        t = lax.broadcasted_iota(jnp.int32, (LANES, seq), 1)
        avg = jnp.where(t // tq == r, 1.0 / tq, 0.0).astype(BF16)
        for p in range(N_PAIRS):
            kmean_ref[p] = _dot(avg, mk_ref[0, :, 2 * p * LANES:(2 * p + 1) * LANES])

    chains = []
    for p in range(N_PAIRS):
        fq_t = fq_ref[0, 2 * p * LANES:(2 * p + 2) * LANES, :]
        for a in range(2):
            qt_ref[len(chains)] = jnp.where(_head_row_mask(fq_t.shape, a), fq_t, jnp.zeros_like(fq_t))
            chains.append((fk_ref, 2 * p * LANES, fv_ref, (2 * p + a) * V_ROWS))
    for p in range(N_PAIRS):
        mq_t = mq_ref[0, p * LANES:(p + 1) * LANES, :]
        kmean = kmean_ref[p].astype(BF16)
        for a in range(2):
            qa_t = jnp.where(_head_row_mask(mq_t.shape, a), mq_t, jnp.zeros_like(mq_t))
            qt_ref[len(chains)] = jnp.concatenate([qa_t, _moba_bias_t(qa_t, kmean, qi, nb)], axis=0)
            chains.append((mk_ref, 2 * p * LANES, mv_ref, (2 * p + a) * V_ROWS))
    n = len(chains)

    def tile_step(start, ms, mask=None, first=False):
        tile_max = []
        for c, (k_ref, k_off, _, _) in enumerate(chains):
            s_t = _dot(k_ref[0, pl.ds(start, tq), k_off:k_off + 2 * LANES], qt_ref[c])
            if mask is not None:
                s_t = jnp.where(mask, s_t, -jnp.inf)
            s_ref[c] = s_t
            tile_max.append(jnp.max(s_t, axis=0, keepdims=True))
        new_ms = []
        for c, (m, (_, _, v_ref, v_row)) in enumerate(zip(ms, chains)):
            m_new = jnp.maximum(m, tile_max[c])
            new_ms.append(m_new)
            p_t = jnp.exp2(s_ref[c] - m_new).astype(BF16)
            pv = _dot(v_ref[0, v_row:v_row + V_ROWS, pl.ds(start, tq)], p_t)
            acc_ref[c] = pv if first else jnp.exp2(m - m_new) * acc_ref[c] + pv
        return new_ms

    key = lax.broadcasted_iota(jnp.int32, (tq, tq), 0)
    qry = lax.broadcasted_iota(jnp.int32, (tq, tq), 1)
    ms = tile_step(pl.multiple_of(qi * tq, tq), [jnp.full((1, tq), -jnp.inf, F32)] * n,
                   mask=key <= qry, first=True)
    lax.fori_loop(0, qi, lambda j, ms: tile_step(pl.multiple_of(j * tq, tq), ms), ms)

    for kind, g_ref in enumerate((fg_ref, mg_ref)):
        for p in range(N_PAIRS):
            o = []
            for c in (kind * N_HEADS + 2 * p, kind * N_HEADS + 2 * p + 1):
                o.append(acc_ref[c, 0:HEAD_DIM, :] / acc_ref[c, HEAD_DIM:HEAD_DIM + 1, :])
            y = jnp.concatenate(o, axis=0).T
            lanes = slice(p * LANES, (p + 1) * LANES)
            y_ref[0, kind, :, lanes] = (y * g_ref[0, :, lanes].astype(F32)).astype(BF16)


def _attn_core(fq, fk, fv, fg, mq, mk, mv, mg):
    B, S, _ = fk.shape
    tq = MOBA_BLOCK
    n_chains = 2 * N_HEADS
    q_tile = lambda width: pl.BlockSpec((1, tq, width), lambda b, i: (b, i, 0))
    qt_tile = lambda rows: pl.BlockSpec((1, rows, tq), lambda b, i: (b, 0, i))
    k_full = pl.BlockSpec((1, S, 2 * HEAD_WIDTH), lambda b, i: (b, 0, 0))
    v_full = pl.BlockSpec((1, N_HEADS * V_ROWS, S), lambda b, i: (b, 0, 0))
    return pl.pallas_call(
        functools.partial(_attn_core_kernel, seq=S),
        grid=(B, S // tq),
        in_specs=[qt_tile(2 * HEAD_WIDTH), k_full, v_full, q_tile(HEAD_WIDTH),
                  qt_tile(HEAD_WIDTH), k_full, v_full, q_tile(HEAD_WIDTH)],
        out_specs=pl.BlockSpec((1, 2, tq, HEAD_WIDTH), lambda b, i: (b, 0, i, 0)),
        out_shape=jax.ShapeDtypeStruct((B, 2, S, HEAD_WIDTH), BF16),
        scratch_shapes=[pltpu.VMEM((N_PAIRS, LANES, LANES), F32),
                        pltpu.VMEM((n_chains, 2 * LANES, tq), BF16),
                        pltpu.VMEM((n_chains, tq, tq), F32),
                        pltpu.VMEM((n_chains, V_ROWS, tq), F32)],
        compiler_params=pltpu.CompilerParams(
            dimension_semantics=("arbitrary", "arbitrary"), vmem_limit_bytes=VMEM_LIMIT),
        name="attn_core",
    )(fq, fk, fv, fg, mq, mk, mv, mg)


def _attn_out_kernel(y_ref, x_ref, w_ref, g_ref, b_ref, o_ref):
    tm = x_ref.shape[1]
    for r in range(0, tm, tm // 2):
        rows = slice(r, r + tm // 2)
        f = _dot(y_ref[0, 0, rows, :], w_ref[0]) + _dot(y_ref[0, 1, rows, :], w_ref[1])
        z = DEEPNORM_ALPHA * x_ref[0, rows, :] + f
        o_ref[0, rows, :] = _layer_norm(z, g_ref[...], b_ref[...])


def _attn_out_proj(y, x, w_out, ln_g, ln_b, *, tm=512):
    B, S, D = x.shape
    full = lambda shape: pl.BlockSpec(shape, lambda b, s: (0,) * len(shape))
    tile = pl.BlockSpec((1, tm, D), lambda b, s: (b, s, 0))
    return pl.pallas_call(
        _attn_out_kernel,
        grid=(B, S // tm),
        in_specs=[pl.BlockSpec((1, 2, tm, HEAD_WIDTH), lambda b, s: (b, 0, s, 0)), tile,
                  full(w_out.shape), full(ln_g.shape), full(ln_b.shape)],
        out_specs=tile,
        out_shape=jax.ShapeDtypeStruct((B, S, D), F32),
        compiler_params=pltpu.CompilerParams(
            dimension_semantics=("arbitrary", "arbitrary"), vmem_limit_bytes=VMEM_LIMIT),
        name="attn_out_proj",
    )(y, x, w_out, ln_g, ln_b)


def _pool_kernel(x_ref, win_ref, wout_ref, g_ref, b_ref, o_ref, ubuf_ref, *, tm, sub):
    s = pl.program_id(1)

    @pl.when(s == 0)
    def _():
        ubuf_ref[0:POOL_HALO, :] = jnp.zeros((POOL_HALO, POOL_WIDTH), F32)

    @pl.when(s > 0)
    def _():
        ubuf_ref[0:POOL_HALO, :] = ubuf_ref[tm:tm + POOL_HALO, :]

    n_groups = len(POOL_WINDOWS)
    for r0 in range(0, tm, sub):
        x = x_ref[0, r0:r0 + sub, :]
        xb = x.astype(BF16)
        t = s * tm + r0 + lax.broadcasted_iota(jnp.int32, (sub, LANES), 0)

        def project(g):
            cols = slice(g * POOL_GROUP, (g + 1) * POOL_GROUP)
            gcols = slice(POOL_WIDTH + g * POOL_GROUP, POOL_WIDTH + (g + 1) * POOL_GROUP)
            return _dot(xb, win_ref[:, cols]), _dot(xb, win_ref[:, gcols])

        def mix(g, u, gate):
            w = POOL_WINDOWS[g]
            cols = slice(g * POOL_GROUP, (g + 1) * POOL_GROUP)
            ubuf_ref[POOL_HALO + r0:POOL_HALO + r0 + sub, cols] = u
            win = ubuf_ref[r0:r0 + POOL_HALO + sub, cols]
            span = 1
            while span < w:
                win = win + pltpu.roll(win, span, 0)
                span *= 2
            inv = 1.0 / jnp.minimum(t + 1, w).astype(F32)
            inv = jnp.concatenate([inv] * (POOL_GROUP // LANES), axis=-1)
            pooled = win[POOL_HALO:, :] * inv - u
            return _dot((pooled * _silu(gate)).astype(BF16), wout_ref[cols, :])

        ahead = project(0)
        f = jnp.zeros((sub, D_MODEL), F32)
        for g in range(n_groups):
            cur = ahead
            if g + 1 < n_groups:
                ahead = project(g + 1)
            f = f + mix(g, *cur)
        z = DEEPNORM_ALPHA * x + f
        o_ref[0, r0:r0 + sub, :] = _layer_norm(z, g_ref[...], b_ref[...])


def _pool_layer(x, w_in, w_out, ln_g, ln_b, *, tm=512, sub=256):
    B, S, D = x.shape
    full = lambda shape: pl.BlockSpec(shape, lambda b, s: (0,) * len(shape), pipeline_mode=pl.Buffered(1))
    tile = pl.BlockSpec((1, tm, D), lambda b, s: (b, s, 0))
    return pl.pallas_call(
        functools.partial(_pool_kernel, tm=tm, sub=sub),
        grid=(B, S // tm),
        in_specs=[tile, full(w_in.shape), full(w_out.shape), full(ln_g.shape), full(ln_b.shape)],
        out_specs=tile,
        out_shape=jax.ShapeDtypeStruct((B, S, D), F32),
        scratch_shapes=[pltpu.VMEM((POOL_HALO + tm, POOL_WIDTH), F32)],
        compiler_params=pltpu.CompilerParams(
            dimension_semantics=("arbitrary", "arbitrary"), vmem_limit_bytes=VMEM_LIMIT),
        name="pool_layer",
    )(x, w_in, w_out, ln_g, ln_b)


def _split2(v):
    hi = v.astype(BF16)
    return hi, (v - hi.astype(F32)).astype(BF16)


def _pool_fold_kernel(win_ref, wgrp_ref, scale_ref, o_ref):
    i = pl.program_id(0)
    n_groups = len(POOL_WINDOWS)

    @pl.when(i < n_groups)
    def _():
        a_hi, a_lo = _split2(win_ref[...])
        b_hi, b_lo = _split2(wgrp_ref[0])
        prod = (_dot(a_hi, b_hi) + _dot(a_hi, b_lo)) + _dot(a_lo, b_hi)
        o_ref[...] = (prod * scale_ref[...]).astype(BF16)

    @pl.when(i >= n_groups)
    def _():
        o_ref[...] = win_ref[...].astype(BF16)


def _pool_fold_weights(w_in, w_grp, scale):
    n_groups = len(POOL_WINDOWS)
    D, width = w_in.shape
    return pl.pallas_call(
        _pool_fold_kernel,
        grid=(width // POOL_GROUP,),
        in_specs=[pl.BlockSpec((D, POOL_GROUP), lambda i: (0, i)),
                  pl.BlockSpec((1, POOL_GROUP, POOL_GROUP), lambda i: (jnp.minimum(i, n_groups - 1), 0, 0)),
                  pl.BlockSpec((1, POOL_GROUP), lambda i: (0, jnp.minimum(i, n_groups - 1)))],
        out_specs=pl.BlockSpec((D, POOL_GROUP), lambda i: (0, i)),
        out_shape=jax.ShapeDtypeStruct((D, width), BF16),
        compiler_params=pltpu.CompilerParams(dimension_semantics=("arbitrary",), vmem_limit_bytes=VMEM_LIMIT),
        name="pool_fold_weights",
    )(w_in, w_grp, scale)


def _attn_layer(x, w_in, b_f, w_out, ln_g, ln_b):
    W = HEAD_WIDTH
    n_ff = N_HEADS
    w_main = jnp.concatenate([w_in[:, :W] * Q_SCALE, w_in[:, W:4 * W],
                              w_in[:, 4 * W + n_ff:5 * W + n_ff] * Q_SCALE, w_in[:, 5 * W + n_ff:]],
                             axis=1).astype(BF16)
    w_ff = jnp.pad(w_in[:, 4 * W:4 * W + n_ff], ((0, 0), (0, LANES - n_ff))).astype(BF16)
    b_pad = jnp.pad(b_f, (0, LANES - n_ff)).reshape(1, LANES)
    y = _attn_core(*_attn_in_proj(x, w_main, w_ff, b_pad))
    return _attn_out_proj(y, x, w_out.reshape(2, W, D_MODEL).astype(BF16),
                          ln_g.reshape(1, -1), ln_b.reshape(1, -1))


def kernel(x, attn_w_in, attn_b_f, attn_w_out, pool_w_in, pool_w_grp, pool_scale, pool_w_out, ln_g, ln_b):
    for layer in range(DEPTH):
        j = layer // 2
        if layer % 2 == 0:
            x = _attn_layer(x, attn_w_in[j], attn_b_f[j], attn_w_out[j], ln_g[layer], ln_b[layer])
        else:
            w_in = _pool_fold_weights(pool_w_in[j], pool_w_grp[j], pool_scale[j].reshape(1, -1))
            x = _pool_layer(x, w_in, pool_w_out[j].astype(BF16),
                            ln_g[layer].reshape(1, -1), ln_b[layer].reshape(1, -1))
    return x
```

```python
import functools
import math

import jax
import jax.numpy as jnp
import numpy as np
from jax import lax
from jax.experimental import pallas as pl
from jax.experimental.pallas import tpu as pltpu

D_MODEL = 1024
DEPTH = 4
HEAD_DIM = 64
N_HEADS = 8
HEAD_WIDTH = N_HEADS * HEAD_DIM
N_PAIRS = N_HEADS // 2
MOBA_BLOCK = 256
MOBA_TOPK = 3
POOL_WIDTH = 2048
POOL_WINDOWS = (2, 4, 8, 16)
POOL_GROUP = 512
POOL_HALO = 16
DEEPNORM_ALPHA = (2 * DEPTH) ** 0.25
LN_EPS = 1e-5
LOG2E = math.log2(math.e)
Q_SCALE = HEAD_DIM ** -0.5 * LOG2E

LANES = 128
SUBLANES = 8
BF16_SUBLANES = 2 * SUBLANES
V_ROWS = HEAD_DIM + BF16_SUBLANES
MASK_BIAS = -30000.0
VMEM_LIMIT = 56 * 1024 * 1024

BF16 = jnp.bfloat16
F32 = jnp.float32


def _dot(a, b):
    return jnp.dot(a, b, preferred_element_type=F32)


def _split3(v):
    hi = v.astype(BF16)
    r1 = v - hi.astype(F32)
    mid = r1.astype(BF16)
    lo = (r1 - mid.astype(F32)).astype(BF16)
    return hi, mid, lo


def _layer_norm(z, g, b):
    mu = jnp.mean(z, axis=-1, keepdims=True)
    zc = z - mu
    var = jnp.mean(zc * zc, axis=-1, keepdims=True)
    return zc * lax.rsqrt(var + LN_EPS) * g + b


def _silu(v):
    return v * jax.nn.sigmoid(v)


def _pack3(v):
    lane = lax.broadcasted_iota(jnp.int32, v.shape, 1)
    hi, mid, lo = (part.astype(F32) for part in _split3(v))
    packed = jnp.where(lane < N_HEADS, hi,
                       jnp.where(lane < 2 * N_HEADS, pltpu.roll(mid, N_HEADS, 1),
                                 jnp.where(lane < 3 * N_HEADS, pltpu.roll(lo, 2 * N_HEADS, 1), 0.0)))
    return packed.astype(BF16)


def _attn_in_kernel(x_ref, w_ref, wff_ref, bf_ref, place_ref, const_ref,
                    fq_ref, fk_ref, fv_ref, fg_ref, mq_ref, mk_ref, mv_ref, mg_ref,
                    carry_ref, *, tm, sub):
    s = pl.program_id(1)

    @pl.when(s == 0)
    def _():
        carry_ref[...] = jnp.zeros_like(carry_ref)

    weights = (w_ref, wff_ref, bf_ref, place_ref, const_ref)
    outs = (fq_ref, fk_ref, fv_ref, fg_ref, mq_ref, mk_ref, mv_ref, mg_ref)
    for r0 in range(0, tm, sub):
        _attn_in_subtile(x_ref, weights, outs, carry_ref, s * tm + r0, slice(r0, r0 + sub))


def _attn_in_subtile(x_ref, weights, outs, carry_ref, t0, rows):
    w_ref, wff_ref, bf_ref, place_ref, const_ref = weights
    fq_ref, fk_ref, fv_ref, fg_ref, mq_ref, mk_ref, mv_ref, mg_ref = outs
    tm = rows.stop - rows.start
    xb = x_ref[0, rows, :].astype(BF16)
    W = HEAD_WIDTH

    def project(i):
        return _dot(xb, w_ref[:, i * W:(i + 1) * W])

    blk = (t0 + lax.broadcasted_iota(jnp.int32, (tm, LANES), 0)) // MOBA_BLOCK
    onehot = jnp.where(lax.broadcasted_iota(jnp.int32, (tm, LANES), 1) == blk, 1.0, 0.0).astype(BF16)

    def pair_lanes(p):
        return slice(2 * p * LANES, (2 * p + 1) * LANES), slice((2 * p + 1) * LANES, (2 * p + 2) * LANES)

    def store_values(v_ref, h):
        ones = jnp.ones((V_ROWS - HEAD_DIM, tm), BF16)
        for p in range(N_PAIRS):
            vt = h[:, p * LANES:(p + 1) * LANES].T.astype(BF16)
            for a in range(2):
                h0 = (2 * p + a) * V_ROWS
                v_ref[0, h0:h0 + HEAD_DIM, rows] = vt[a * HEAD_DIM:(a + 1) * HEAD_DIM, :]
                v_ref[0, h0 + HEAD_DIM:h0 + V_ROWS, rows] = ones

    ff = _dot(xb, wff_ref[...]) + bf_ref[...]
    log_f_parts = _pack3(jax.nn.log_sigmoid(ff))
    store_values(fv_ref, project(2))
    fg_ref[0, rows, :] = _silu(project(3)).astype(BF16)

    row = lax.broadcasted_iota(jnp.int32, (tm, tm), 0)
    col = lax.broadcasted_iota(jnp.int32, (tm, tm), 1)
    tri = jnp.where(col <= row, 1.0, 0.0).astype(BF16)
    parts = _dot(tri, log_f_parts)
    c = (parts + pltpu.roll(parts, LANES - N_HEADS, 1)) + pltpu.roll(parts, LANES - 2 * N_HEADS, 1)
    c = c + carry_ref[0:1, :]
    carry_ref[0:1, :] = c[tm - 1:tm, :]
    c_parts = _pack3(c * LOG2E)
    h = project(4)
    for p in range(N_PAIRS):
        mq_ref[0, p * LANES:(p + 1) * LANES, rows] = h[:, p * LANES:(p + 1) * LANES].T.astype(BF16)
    h = project(5)
    for p in range(N_PAIRS):
        lo_lanes, hi_lanes = pair_lanes(p)
        mk_ref[0, rows, lo_lanes] = h[:, p * LANES:(p + 1) * LANES].astype(BF16)
        mk_ref[0, rows, hi_lanes] = onehot
    ext = _dot(c_parts, place_ref[...]) + const_ref[...]

    store_values(mv_ref, project(6))
    mg_ref[0, rows, :] = _silu(project(7)).astype(BF16)
    h = project(0)
    for p in range(N_PAIRS):
        lo_lanes, hi_lanes = pair_lanes(p)
        fq_ref[0, lo_lanes, rows] = h[:, p * LANES:(p + 1) * LANES].T.astype(BF16)
        fq_ref[0, hi_lanes, rows] = ext[:, p * LANES:(p + 1) * LANES].T.astype(BF16)
    h = project(1)
    for p in range(N_PAIRS):
        lo_lanes, hi_lanes = pair_lanes(p)
        fk_ref[0, rows, lo_lanes] = h[:, p * LANES:(p + 1) * LANES].astype(BF16)
        fk_ref[0, rows, hi_lanes] = ext[:, W + p * LANES:W + (p + 1) * LANES].astype(BF16)


def _decay_placement():
    place = np.zeros((LANES, 2 * HEAD_WIDTH), np.float32)
    const = np.zeros((1, 2 * HEAD_WIDTH), np.float32)
    for h in range(N_HEADS):
        p, a = divmod(h, 2)
        base_q = p * LANES + a * HEAD_DIM
        base_k = HEAD_WIDTH + p * LANES + a * HEAD_DIM
        for r in range(3):
            place[r * N_HEADS + h, base_q + r] = 1.0
            const[0, base_q + 3 + r] = 1.0
            const[0, base_k + r] = 1.0
            place[r * N_HEADS + h, base_k + 3 + r] = -1.0
    return jnp.asarray(place, BF16), jnp.asarray(const, F32)


def _layer_slab(stacked, layer):
    tail = stacked.shape[1:]
    return pl.BlockSpec((None,) + tail, lambda *_: (layer,) + (0,) * len(tail), pipeline_mode=pl.Buffered(1))


def _attn_in_proj(x, w_main, w_ff, b_f, layer, *, tm=512, sub=256):
    B, S, D = x.shape
    place, const = _decay_placement()
    W = HEAD_WIDTH
    full = lambda shape: pl.BlockSpec(shape, lambda b, s: (0,) * len(shape))
    tile = lambda width: pl.BlockSpec((1, tm, width), lambda b, s: (b, s, 0))
    row_major = lambda w: (tile(w), jax.ShapeDtypeStruct((B, S, w), BF16))
    transposed = lambda rows: (pl.BlockSpec((1, rows, tm), lambda b, s: (b, 0, s)),
                               jax.ShapeDtypeStruct((B, rows, S), BF16))
    values = transposed(N_HEADS * V_ROWS)
    outs = (transposed(2 * W), row_major(2 * W), values, row_major(W),
            transposed(W), row_major(2 * W), values, row_major(W))
    return pl.pallas_call(
        functools.partial(_attn_in_kernel, tm=tm, sub=sub),
        grid=(B, S // tm),
        in_specs=[tile(D), _layer_slab(w_main, layer), _layer_slab(w_ff, layer), _layer_slab(b_f, layer),
                  full(place.shape), full(const.shape)],
        out_specs=[o[0] for o in outs],
        out_shape=[o[1] for o in outs],
        scratch_shapes=[pltpu.VMEM((8, LANES), F32)],
        compiler_params=pltpu.CompilerParams(
            dimension_semantics=("arbitrary", "arbitrary"), vmem_limit_bytes=VMEM_LIMIT),
        name="attn_in_proj",
    )(x, w_main, w_ff, b_f, place, const)


def _head_row_mask(shape, a):
    row = lax.broadcasted_iota(jnp.int32, shape, 0)
    return (row // HEAD_DIM) % 2 == a


def _moba_bias_t(qa_t, kmean, qi, nb):
    tq = qa_t.shape[1]
    gate = _dot(kmean, qa_t)[0:SUBLANES, :]
    blk = lax.broadcasted_iota(jnp.int32, gate.shape, 0)
    gate = jnp.where(blk < qi, gate, -jnp.inf)
    rank = jnp.zeros(gate.shape, jnp.int32)
    for d in range(1, nb):
        other = pltpu.roll(gate, d, 0)
        rank = rank + jnp.where(blk >= d, jnp.where(other >= gate, 1, 0), jnp.where(other > gate, 1, 0))
    keep = ((rank < MOBA_TOPK) & (gate > -jnp.inf)) | (blk == qi)
    bias_t = jnp.where(keep, 0.0, MASK_BIAS)
    return jnp.concatenate([bias_t, jnp.zeros((LANES - SUBLANES, tq), F32)], axis=0).astype(BF16)


def _attn_core_kernel(fq_ref, fk_ref, fv_ref, fg_ref, mq_ref, mk_ref, mv_ref, mg_ref, y_ref,
                      kmean_ref, qt_ref, s_ref, acc_ref, *, seq):
    tq = MOBA_BLOCK
    nb = seq // tq
    assert nb == SUBLANES
    qi = pl.program_id(1)

    @pl.when(qi == 0)
    def _():
        r = lax.broadcasted_iota(jnp.int32, (LANES, seq), 0)
        t = lax.broadcasted_iota(jnp.int32, (LANES, seq), 1)
        avg = jnp.where(t // tq == r, 1.0 / tq, 0.0).astype(BF16)
        for p in range(N_PAIRS):
            kmean_ref[p] = _dot(avg, mk_ref[0, :, 2 * p * LANES:(2 * p + 1) * LANES])

    chains = []
    for p in range(N_PAIRS):
        fq_t = fq_ref[0, 2 * p * LANES:(2 * p + 2) * LANES, :]
        for a in range(2):
            qt_ref[len(chains)] = jnp.where(_head_row_mask(fq_t.shape, a), fq_t, jnp.zeros_like(fq_t))
            chains.append((fk_ref, 2 * p * LANES, fv_ref, (2 * p + a) * V_ROWS))
    for p in range(N_PAIRS):
        mq_t = mq_ref[0, p * LANES:(p + 1) * LANES, :]
        kmean = kmean_ref[p].astype(BF16)
        for a in range(2):
            qa_t = jnp.where(_head_row_mask(mq_t.shape, a), mq_t, jnp.zeros_like(mq_t))
            qt_ref[len(chains)] = jnp.concatenate([qa_t, _moba_bias_t(qa_t, kmean, qi, nb)], axis=0)
            chains.append((mk_ref, 2 * p * LANES, mv_ref, (2 * p + a) * V_ROWS))
    n = len(chains)

    def tile_step(start, ms, mask=None, first=False):
        tile_max = []
        for c, (k_ref, k_off, _, _) in enumerate(chains):
            s_t = _dot(k_ref[0, pl.ds(start, tq), k_off:k_off + 2 * LANES], qt_ref[c])
            if mask is not None:
                s_t = jnp.where(mask, s_t, -jnp.inf)
            s_ref[c] = s_t
            tile_max.append(jnp.max(s_t, axis=0, keepdims=True))
        new_ms = []
        for c, (m, (_, _, v_ref, v_row)) in enumerate(zip(ms, chains)):
            m_new = jnp.maximum(m, tile_max[c])
            new_ms.append(m_new)
            p_t = jnp.exp2(s_ref[c] - m_new).astype(BF16)
            pv = _dot(v_ref[0, v_row:v_row + V_ROWS, pl.ds(start, tq)], p_t)
            acc_ref[c] = pv if first else jnp.exp2(m - m_new) * acc_ref[c] + pv
        return new_ms

    key = lax.broadcasted_iota(jnp.int32, (tq, tq), 0)
    qry = lax.broadcasted_iota(jnp.int32, (tq, tq), 1)
    ms = tile_step(pl.multiple_of(qi * tq, tq), [jnp.full((1, tq), -jnp.inf, F32)] * n,
                   mask=key <= qry, first=True)
    lax.fori_loop(0, qi, lambda j, ms: tile_step(pl.multiple_of(j * tq, tq), ms), ms)

    for kind, g_ref in enumerate((fg_ref, mg_ref)):
        for p in range(N_PAIRS):
            o = []
            for c in (kind * N_HEADS + 2 * p, kind * N_HEADS + 2 * p + 1):
                o.append(acc_ref[c, 0:HEAD_DIM, :] / acc_ref[c, HEAD_DIM:HEAD_DIM + 1, :])
            y = jnp.concatenate(o, axis=0).T
            lanes = slice(p * LANES, (p + 1) * LANES)
            y_ref[0, kind, :, lanes] = (y * g_ref[0, :, lanes].astype(F32)).astype(BF16)


def _attn_core(fq, fk, fv, fg, mq, mk, mv, mg):
    B, S, _ = fk.shape
    tq = MOBA_BLOCK
    n_chains = 2 * N_HEADS
    q_tile = lambda width: pl.BlockSpec((1, tq, width), lambda b, i: (b, i, 0))
    qt_tile = lambda rows: pl.BlockSpec((1, rows, tq), lambda b, i: (b, 0, i))
    k_full = pl.BlockSpec((1, S, 2 * HEAD_WIDTH), lambda b, i: (b, 0, 0))
    v_full = pl.BlockSpec((1, N_HEADS * V_ROWS, S), lambda b, i: (b, 0, 0))
    return pl.pallas_call(
        functools.partial(_attn_core_kernel, seq=S),
        grid=(B, S // tq),
        in_specs=[qt_tile(2 * HEAD_WIDTH), k_full, v_full, q_tile(HEAD_WIDTH),
                  qt_tile(HEAD_WIDTH), k_full, v_full, q_tile(HEAD_WIDTH)],
        out_specs=pl.BlockSpec((1, 2, tq, HEAD_WIDTH), lambda b, i: (b, 0, i, 0)),
        out_shape=jax.ShapeDtypeStruct((B, 2, S, HEAD_WIDTH), BF16),
        scratch_shapes=[pltpu.VMEM((N_PAIRS, LANES, LANES), F32),
                        pltpu.VMEM((n_chains, 2 * LANES, tq), BF16),
                        pltpu.VMEM((n_chains, tq, tq), F32),
                        pltpu.VMEM((n_chains, V_ROWS, tq), F32)],
        compiler_params=pltpu.CompilerParams(
            dimension_semantics=("arbitrary", "arbitrary"), vmem_limit_bytes=VMEM_LIMIT),
        name="attn_core",
    )(fq, fk, fv, fg, mq, mk, mv, mg)


def _attn_out_kernel(y_ref, x_ref, w_ref, g_ref, b_ref, o_ref):
    tm = x_ref.shape[1]
    for r in range(0, tm, tm // 2):
        rows = slice(r, r + tm // 2)
        f = _dot(y_ref[0, 0, rows, :], w_ref[0]) + _dot(y_ref[0, 1, rows, :], w_ref[1])
        z = DEEPNORM_ALPHA * x_ref[0, rows, :] + f
        o_ref[0, rows, :] = _layer_norm(z, g_ref[...], b_ref[...])


def _attn_out_proj(y, x, w_out, ln_g, ln_b, attn_layer, layer, *, tm=512):
    B, S, D = x.shape
    tile = pl.BlockSpec((1, tm, D), lambda b, s: (b, s, 0))
    return pl.pallas_call(
        _attn_out_kernel,
        grid=(B, S // tm),
        in_specs=[pl.BlockSpec((1, 2, tm, HEAD_WIDTH), lambda b, s: (b, 0, s, 0)), tile,
                  _layer_slab(w_out, attn_layer), _layer_slab(ln_g, layer), _layer_slab(ln_b, layer)],
        out_specs=tile,
        out_shape=jax.ShapeDtypeStruct((B, S, D), F32),
        compiler_params=pltpu.CompilerParams(
            dimension_semantics=("arbitrary", "arbitrary"), vmem_limit_bytes=VMEM_LIMIT),
        name="attn_out_proj",
    )(y, x, w_out, ln_g, ln_b)


def _pool_kernel(x_ref, win_ref, wout_ref, g_ref, b_ref, o_ref, ubuf_ref, *, tm, sub):
    s = pl.program_id(1)

    @pl.when(s == 0)
    def _():
        ubuf_ref[0:POOL_HALO, :] = jnp.zeros((POOL_HALO, POOL_WIDTH), F32)

    @pl.when(s > 0)
    def _():
        ubuf_ref[0:POOL_HALO, :] = ubuf_ref[tm:tm + POOL_HALO, :]

    n_groups = len(POOL_WINDOWS)
    for r0 in range(0, tm, sub):
        x = x_ref[0, r0:r0 + sub, :]
        xb = x.astype(BF16)
        t = s * tm + r0 + lax.broadcasted_iota(jnp.int32, (sub, LANES), 0)

        def project(g):
            cols = slice(g * POOL_GROUP, (g + 1) * POOL_GROUP)
            gcols = slice(POOL_WIDTH + g * POOL_GROUP, POOL_WIDTH + (g + 1) * POOL_GROUP)
            return _dot(xb, win_ref[:, cols]), _dot(xb, win_ref[:, gcols])

        def mix(g, u, gate):
            w = POOL_WINDOWS[g]
            cols = slice(g * POOL_GROUP, (g + 1) * POOL_GROUP)
            ubuf_ref[POOL_HALO + r0:POOL_HALO + r0 + sub, cols] = u
            win = ubuf_ref[r0:r0 + POOL_HALO + sub, cols]
            span = 1
            while span < w:
                win = win + pltpu.roll(win, span, 0)
                span *= 2
            inv = 1.0 / jnp.minimum(t + 1, w).astype(F32)
            inv = jnp.concatenate([inv] * (POOL_GROUP // LANES), axis=-1)
            pooled = win[POOL_HALO:, :] * inv - u
            return _dot((pooled * _silu(gate)).astype(BF16), wout_ref[cols, :])

        ahead = project(0)
        f = jnp.zeros((sub, D_MODEL), F32)
        for g in range(n_groups):
            cur = ahead
            if g + 1 < n_groups:
                ahead = project(g + 1)
            f = f + mix(g, *cur)
        z = DEEPNORM_ALPHA * x + f
        o_ref[0, r0:r0 + sub, :] = _layer_norm(z, g_ref[...], b_ref[...])


def _pool_layer(x, w_in, w_out, ln_g, ln_b, pool_layer, layer, *, tm=512, sub=256):
    B, S, D = x.shape
    full = lambda shape: pl.BlockSpec(shape, lambda b, s: (0,) * len(shape), pipeline_mode=pl.Buffered(1))
    tile = pl.BlockSpec((1, tm, D), lambda b, s: (b, s, 0))
    return pl.pallas_call(
        functools.partial(_pool_kernel, tm=tm, sub=sub),
        grid=(B, S // tm),
        in_specs=[tile, full(w_in.shape), _layer_slab(w_out, pool_layer),
                  _layer_slab(ln_g, layer), _layer_slab(ln_b, layer)],
        out_specs=tile,
        out_shape=jax.ShapeDtypeStruct((B, S, D), F32),
        scratch_shapes=[pltpu.VMEM((POOL_HALO + tm, POOL_WIDTH), F32)],
        compiler_params=pltpu.CompilerParams(
            dimension_semantics=("arbitrary", "arbitrary"), vmem_limit_bytes=VMEM_LIMIT),
        name="pool_layer",
    )(x, w_in, w_out, ln_g, ln_b)


def _split2(v):
    hi = v.astype(BF16)
    return hi, (v - hi.astype(F32)).astype(BF16)


def _pool_fold_kernel(win_ref, wgrp_ref, scale_ref, o_ref):
    i = pl.program_id(0)
    n_groups = len(POOL_WINDOWS)

    @pl.when(i < n_groups)
    def _():
        a_hi, a_lo = _split2(win_ref[...])
        b_hi, b_lo = _split2(wgrp_ref[0])
        prod = (_dot(a_hi, b_hi) + _dot(a_hi, b_lo)) + _dot(a_lo, b_hi)
        o_ref[...] = (prod * scale_ref[...]).astype(BF16)

    @pl.when(i >= n_groups)
    def _():
        o_ref[...] = win_ref[...].astype(BF16)


def _pool_fold_weights(w_in, w_grp, scale, pool_layer):
    n_groups = len(POOL_WINDOWS)
    _, D, width = w_in.shape
    group = lambda i: jnp.minimum(i, n_groups - 1)
    return pl.pallas_call(
        _pool_fold_kernel,
        grid=(width // POOL_GROUP,),
        in_specs=[pl.BlockSpec((None, D, POOL_GROUP), lambda i: (pool_layer, 0, i)),
                  pl.BlockSpec((None, 1, POOL_GROUP, POOL_GROUP), lambda i: (pool_layer, group(i), 0, 0)),
                  pl.BlockSpec((None, 1, POOL_GROUP), lambda i: (pool_layer, 0, group(i)))],
        out_specs=pl.BlockSpec((D, POOL_GROUP), lambda i: (0, i)),
        out_shape=jax.ShapeDtypeStruct((D, width), BF16),
        compiler_params=pltpu.CompilerParams(dimension_semantics=("arbitrary",), vmem_limit_bytes=VMEM_LIMIT),
        name="pool_fold_weights",
    )(w_in, w_grp, scale)


def kernel(x, attn_w_in, attn_b_f, attn_w_out, pool_w_in, pool_w_grp, pool_scale, pool_w_out, ln_g, ln_b):
    W = HEAD_WIDTH
    n_ff = N_HEADS
    w_main = jnp.concatenate([attn_w_in[..., :W] * Q_SCALE, attn_w_in[..., W:4 * W],
                              attn_w_in[..., 4 * W + n_ff:5 * W + n_ff] * Q_SCALE,
                              attn_w_in[..., 5 * W + n_ff:]], axis=-1).astype(BF16)
    w_ff = jnp.pad(attn_w_in[..., 4 * W:4 * W + n_ff], ((0, 0), (0, 0), (0, LANES - n_ff))).astype(BF16)
    b_pad = jnp.pad(attn_b_f, ((0, 0), (0, LANES - n_ff)))[:, None, :]
    w_attn_out = attn_w_out.reshape(-1, 2, W, D_MODEL).astype(BF16)
    w_pool_out = pool_w_out.astype(BF16)
    scale = pool_scale[:, None, :]
    ln_g = ln_g[:, None, :]
    ln_b = ln_b[:, None, :]

    for layer in range(DEPTH):
        j = layer // 2
        if layer % 2 == 0:
            y = _attn_core(*_attn_in_proj(x, w_main, w_ff, b_pad, j))
            x = _attn_out_proj(y, x, w_attn_out, ln_g, ln_b, j, layer)
        else:
            w_in = _pool_fold_weights(pool_w_in, pool_w_grp, scale, j)
            x = _pool_layer(x, w_in, w_pool_out, ln_g, ln_b, j, layer)
    return x
```

```python
import functools
import math

import jax
import jax.numpy as jnp
import numpy as np
from jax import lax
from jax.experimental import pallas as pl
from jax.experimental.pallas import tpu as pltpu

D_MODEL = 1024
DEPTH = 4
HEAD_DIM = 64
N_HEADS = 8
HEAD_WIDTH = N_HEADS * HEAD_DIM
N_PAIRS = N_HEADS // 2
MOBA_BLOCK = 256
MOBA_TOPK = 3
POOL_WIDTH = 2048
POOL_WINDOWS = (2, 4, 8, 16)
POOL_GROUP = 512
POOL_HALO = 16
DEEPNORM_ALPHA = (2 * DEPTH) ** 0.25
LN_EPS = 1e-5
LOG2E = math.log2(math.e)
Q_SCALE = HEAD_DIM ** -0.5 * LOG2E

LANES = 128
SUBLANES = 8
BF16_SUBLANES = 2 * SUBLANES
V_ROWS = HEAD_DIM + BF16_SUBLANES
MASK_BIAS = -30000.0
VMEM_LIMIT = 56 * 1024 * 1024

BF16 = jnp.bfloat16
F32 = jnp.float32


def _dot(a, b):
    return jnp.dot(a, b, preferred_element_type=F32)


def _split3(v):
    hi = v.astype(BF16)
    r1 = v - hi.astype(F32)
    mid = r1.astype(BF16)
    lo = (r1 - mid.astype(F32)).astype(BF16)
    return hi, mid, lo


def _layer_norm(z, g, b):
    mu = jnp.mean(z, axis=-1, keepdims=True)
    zc = z - mu
    var = jnp.mean(zc * zc, axis=-1, keepdims=True)
    return zc * lax.rsqrt(var + LN_EPS) * g + b


def _silu(v):
    return v * jax.nn.sigmoid(v)


def _pack3(v):
    lane = lax.broadcasted_iota(jnp.int32, v.shape, 1)
    hi, mid, lo = (part.astype(F32) for part in _split3(v))
    packed = jnp.where(lane < N_HEADS, hi,
                       jnp.where(lane < 2 * N_HEADS, pltpu.roll(mid, N_HEADS, 1),
                                 jnp.where(lane < 3 * N_HEADS, pltpu.roll(lo, 2 * N_HEADS, 1), 0.0)))
    return packed.astype(BF16)


def _attn_in_kernel(x_ref, w_ref, wff_ref, bf_ref, place_ref, const_ref,
                    fq_ref, fk_ref, fv_ref, fg_ref, mq_ref, mk_ref, mv_ref, mg_ref,
                    carry_ref, *, tm, sub):
    s = pl.program_id(1)

    @pl.when(s == 0)
    def _():
        carry_ref[...] = jnp.zeros_like(carry_ref)

    weights = (w_ref, wff_ref, bf_ref, place_ref, const_ref)
    outs = (fq_ref, fk_ref, fv_ref, fg_ref, mq_ref, mk_ref, mv_ref, mg_ref)
    for r0 in range(0, tm, sub):
        _attn_in_subtile(x_ref, weights, outs, carry_ref, s * tm + r0, slice(r0, r0 + sub))


def _attn_in_subtile(x_ref, weights, outs, carry_ref, t0, rows):
    w_ref, wff_ref, bf_ref, place_ref, const_ref = weights
    fq_ref, fk_ref, fv_ref, fg_ref, mq_ref, mk_ref, mv_ref, mg_ref = outs
    tm = rows.stop - rows.start
    xb = x_ref[0, rows, :].astype(BF16)
    W = HEAD_WIDTH

    def project(i):
        return _dot(xb, w_ref[:, i * W:(i + 1) * W])

    blk = (t0 + lax.broadcasted_iota(jnp.int32, (tm, LANES), 0)) // MOBA_BLOCK
    onehot = jnp.where(lax.broadcasted_iota(jnp.int32, (tm, LANES), 1) == blk, 1.0, 0.0).astype(BF16)

    def pair_lanes(p):
        return slice(2 * p * LANES, (2 * p + 1) * LANES), slice((2 * p + 1) * LANES, (2 * p + 2) * LANES)

    def store_values(v_ref, h):
        ones = jnp.ones((V_ROWS - HEAD_DIM, tm), BF16)
        for p in range(N_PAIRS):
            vt = h[:, p * LANES:(p + 1) * LANES].T.astype(BF16)
            for a in range(2):
                h0 = (2 * p + a) * V_ROWS
                v_ref[0, h0:h0 + HEAD_DIM, rows] = vt[a * HEAD_DIM:(a + 1) * HEAD_DIM, :]
                v_ref[0, h0 + HEAD_DIM:h0 + V_ROWS, rows] = ones

    ff = _dot(xb, wff_ref[...]) + bf_ref[...]
    log_f_parts = _pack3(jax.nn.log_sigmoid(ff))
    store_values(fv_ref, project(2))
    fg_ref[0, rows, :] = _silu(project(3)).astype(BF16)

    row = lax.broadcasted_iota(jnp.int32, (tm, tm), 0)
    col = lax.broadcasted_iota(jnp.int32, (tm, tm), 1)
    tri = jnp.where(col <= row, 1.0, 0.0).astype(BF16)
    parts = _dot(tri, log_f_parts)
    c = (parts + pltpu.roll(parts, LANES - N_HEADS, 1)) + pltpu.roll(parts, LANES - 2 * N_HEADS, 1)
    c = c + carry_ref[0:1, :]
    carry_ref[0:1, :] = c[tm - 1:tm, :]
    c_parts = _pack3(c * LOG2E)
    h = project(4)
    for p in range(N_PAIRS):
        mq_ref[0, p * LANES:(p + 1) * LANES, rows] = h[:, p * LANES:(p + 1) * LANES].T.astype(BF16)
    h = project(5)
    for p in range(N_PAIRS):
        lo_lanes, hi_lanes = pair_lanes(p)
        mk_ref[0, rows, lo_lanes] = h[:, p * LANES:(p + 1) * LANES].astype(BF16)
        mk_ref[0, rows, hi_lanes] = onehot
    ext = _dot(c_parts, place_ref[...]) + const_ref[...]

    store_values(mv_ref, project(6))
    mg_ref[0, rows, :] = _silu(project(7)).astype(BF16)
    h = project(0)
    for p in range(N_PAIRS):
        lo_lanes, hi_lanes = pair_lanes(p)
        fq_ref[0, lo_lanes, rows] = h[:, p * LANES:(p + 1) * LANES].T.astype(BF16)
        fq_ref[0, hi_lanes, rows] = ext[:, p * LANES:(p + 1) * LANES].T.astype(BF16)
    h = project(1)
    for p in range(N_PAIRS):
        lo_lanes, hi_lanes = pair_lanes(p)
        fk_ref[0, rows, lo_lanes] = h[:, p * LANES:(p + 1) * LANES].astype(BF16)
        fk_ref[0, rows, hi_lanes] = ext[:, W + p * LANES:W + (p + 1) * LANES].astype(BF16)


def _decay_placement():
    place = np.zeros((LANES, 2 * HEAD_WIDTH), np.float32)
    const = np.zeros((1, 2 * HEAD_WIDTH), np.float32)
    for h in range(N_HEADS):
        p, a = divmod(h, 2)
        base_q = p * LANES + a * HEAD_DIM
        base_k = HEAD_WIDTH + p * LANES + a * HEAD_DIM
        for r in range(3):
            place[r * N_HEADS + h, base_q + r] = 1.0
            const[0, base_q + 3 + r] = 1.0
            const[0, base_k + r] = 1.0
            place[r * N_HEADS + h, base_k + 3 + r] = -1.0
    return jnp.asarray(place, BF16), jnp.asarray(const, F32)


def _layer_slab(stacked, layer):
    tail = stacked.shape[1:]
    return pl.BlockSpec((None,) + tail, lambda *_: (layer,) + (0,) * len(tail), pipeline_mode=pl.Buffered(1))


def _attn_in_proj(x, w_main, w_ff, b_f, layer, *, tm=512, sub=256):
    B, S, D = x.shape
    place, const = _decay_placement()
    W = HEAD_WIDTH
    full = lambda shape: pl.BlockSpec(shape, lambda b, s: (0,) * len(shape))
    tile = lambda width: pl.BlockSpec((1, tm, width), lambda b, s: (b, s, 0))
    row_major = lambda w: (tile(w), jax.ShapeDtypeStruct((B, S, w), BF16))
    transposed = lambda rows: (pl.BlockSpec((1, rows, tm), lambda b, s: (b, 0, s)),
                               jax.ShapeDtypeStruct((B, rows, S), BF16))
    values = transposed(N_HEADS * V_ROWS)
    outs = (transposed(2 * W), row_major(2 * W), values, row_major(W),
            transposed(W), row_major(2 * W), values, row_major(W))
    return pl.pallas_call(
        functools.partial(_attn_in_kernel, tm=tm, sub=sub),
        grid=(B, S // tm),
        in_specs=[tile(D), _layer_slab(w_main, layer), _layer_slab(w_ff, layer), _layer_slab(b_f, layer),
                  full(place.shape), full(const.shape)],
        out_specs=[o[0] for o in outs],
        out_shape=[o[1] for o in outs],
        scratch_shapes=[pltpu.VMEM((8, LANES), F32)],
        compiler_params=pltpu.CompilerParams(
            dimension_semantics=("arbitrary", "arbitrary"), vmem_limit_bytes=VMEM_LIMIT),
        name="attn_in_proj",
    )(x, w_main, w_ff, b_f, place, const)


def _head_row_mask(shape, a):
    row = lax.broadcasted_iota(jnp.int32, shape, 0)
    return (row // HEAD_DIM) % 2 == a


def _moba_bias_t(qa_t, kmean, qi, nb):
    tq = qa_t.shape[1]
    gate = _dot(kmean, qa_t)[0:SUBLANES, :]
    blk = lax.broadcasted_iota(jnp.int32, gate.shape, 0)
    gate = jnp.where(blk < qi, gate, -jnp.inf)
    rank = jnp.zeros(gate.shape, jnp.int32)
    for d in range(1, nb):
        other = pltpu.roll(gate, d, 0)
        rank = rank + jnp.where(blk >= d, jnp.where(other >= gate, 1, 0), jnp.where(other > gate, 1, 0))
    keep = ((rank < MOBA_TOPK) & (gate > -jnp.inf)) | (blk == qi)
    bias_t = jnp.where(keep, 0.0, MASK_BIAS)
    return jnp.concatenate([bias_t, jnp.zeros((LANES - SUBLANES, tq), F32)], axis=0).astype(BF16)


def _attn_core_kernel(fq_ref, fk_ref, fv_ref, fg_ref, mq_ref, mk_ref, mv_ref, mg_ref, y_ref,
                      kmean_ref, qt_ref, s_ref, acc_ref, *, seq):
    tq = MOBA_BLOCK
    nb = seq // tq
    assert nb == SUBLANES
    qi = pl.program_id(1)

    @pl.when(qi == 0)
    def _():
        r = lax.broadcasted_iota(jnp.int32, (LANES, seq), 0)
        t = lax.broadcasted_iota(jnp.int32, (LANES, seq), 1)
        avg = jnp.where(t // tq == r, 1.0 / tq, 0.0).astype(BF16)
        for p in range(N_PAIRS):
            kmean_ref[p] = _dot(avg, mk_ref[0, :, 2 * p * LANES:(2 * p + 1) * LANES])

    chains = []
    for p in range(N_PAIRS):
        fq_t = fq_ref[0, 2 * p * LANES:(2 * p + 2) * LANES, :]
        for a in range(2):
            qt_ref[len(chains)] = jnp.where(_head_row_mask(fq_t.shape, a), fq_t, jnp.zeros_like(fq_t))
            chains.append((fk_ref, 2 * p * LANES, fv_ref, (2 * p + a) * V_ROWS))
    for p in range(N_PAIRS):
        mq_t = mq_ref[0, p * LANES:(p + 1) * LANES, :]
        kmean = kmean_ref[p].astype(BF16)
        for a in range(2):
            qa_t = jnp.where(_head_row_mask(mq_t.shape, a), mq_t, jnp.zeros_like(mq_t))
            qt_ref[len(chains)] = jnp.concatenate([qa_t, _moba_bias_t(qa_t, kmean, qi, nb)], axis=0)
            chains.append((mk_ref, 2 * p * LANES, mv_ref, (2 * p + a) * V_ROWS))
    n = len(chains)

    def tile_step(start, ms, mask=None, first=False):
        tile_max = []
        for c, (k_ref, k_off, _, _) in enumerate(chains):
            s_t = _dot(k_ref[0, pl.ds(start, tq), k_off:k_off + 2 * LANES], qt_ref[c])
            if mask is not None:
                s_t = jnp.where(mask, s_t, -jnp.inf)
            s_ref[c] = s_t
            tile_max.append(jnp.max(s_t, axis=0, keepdims=True))
        new_ms = []
        for c, (m, (_, _, v_ref, v_row)) in enumerate(zip(ms, chains)):
            m_new = jnp.maximum(m, tile_max[c])
            new_ms.append(m_new)
            p_t = jnp.exp2(s_ref[c] - m_new).astype(BF16)
            pv = _dot(v_ref[0, v_row:v_row + V_ROWS, pl.ds(start, tq)], p_t)
            acc_ref[c] = pv if first else jnp.exp2(m - m_new) * acc_ref[c] + pv
        return new_ms

    key = lax.broadcasted_iota(jnp.int32, (tq, tq), 0)
    qry = lax.broadcasted_iota(jnp.int32, (tq, tq), 1)
    ms = tile_step(pl.multiple_of(qi * tq, tq), [jnp.full((1, tq), -jnp.inf, F32)] * n,
                   mask=key <= qry, first=True)
    lax.fori_loop(0, qi, lambda j, ms: tile_step(pl.multiple_of(j * tq, tq), ms), ms)

    for kind, g_ref in enumerate((fg_ref, mg_ref)):
        for p in range(N_PAIRS):
            o = []
            for c in (kind * N_HEADS + 2 * p, kind * N_HEADS + 2 * p + 1):
                o.append(acc_ref[c, 0:HEAD_DIM, :] / acc_ref[c, HEAD_DIM:HEAD_DIM + 1, :])
            y = jnp.concatenate(o, axis=0).T
            lanes = slice(p * LANES, (p + 1) * LANES)
            y_ref[0, kind, :, lanes] = (y * g_ref[0, :, lanes].astype(F32)).astype(BF16)


def _attn_core(fq, fk, fv, fg, mq, mk, mv, mg):
    B, S, _ = fk.shape
    tq = MOBA_BLOCK
    n_chains = 2 * N_HEADS
    q_tile = lambda width: pl.BlockSpec((1, tq, width), lambda b, i: (b, i, 0))
    qt_tile = lambda rows: pl.BlockSpec((1, rows, tq), lambda b, i: (b, 0, i))
    k_full = pl.BlockSpec((1, S, 2 * HEAD_WIDTH), lambda b, i: (b, 0, 0))
    v_full = pl.BlockSpec((1, N_HEADS * V_ROWS, S), lambda b, i: (b, 0, 0))
    return pl.pallas_call(
        functools.partial(_attn_core_kernel, seq=S),
        grid=(B, S // tq),
        in_specs=[qt_tile(2 * HEAD_WIDTH), k_full, v_full, q_tile(HEAD_WIDTH),
                  qt_tile(HEAD_WIDTH), k_full, v_full, q_tile(HEAD_WIDTH)],
        out_specs=pl.BlockSpec((1, 2, tq, HEAD_WIDTH), lambda b, i: (b, 0, i, 0)),
        out_shape=jax.ShapeDtypeStruct((B, 2, S, HEAD_WIDTH), BF16),
        scratch_shapes=[pltpu.VMEM((N_PAIRS, LANES, LANES), F32),
                        pltpu.VMEM((n_chains, 2 * LANES, tq), BF16),
                        pltpu.VMEM((n_chains, tq, tq), F32),
                        pltpu.VMEM((n_chains, V_ROWS, tq), F32)],
        compiler_params=pltpu.CompilerParams(
            dimension_semantics=("arbitrary", "arbitrary"), vmem_limit_bytes=VMEM_LIMIT),
        name="attn_core",
    )(fq, fk, fv, fg, mq, mk, mv, mg)


def _pool_kernel(y_ref, x_ref, wa_ref, ga_ref, ba_ref, win_ref, wout_ref, g_ref, b_ref, o_ref, ubuf_ref,
                 *, tm, sub):
    s = pl.program_id(1)

    @pl.when(s == 0)
    def _():
        ubuf_ref[0:POOL_HALO, :] = jnp.zeros((POOL_HALO, POOL_WIDTH), F32)

    @pl.when(s > 0)
    def _():
        ubuf_ref[0:POOL_HALO, :] = ubuf_ref[tm:tm + POOL_HALO, :]

    n_groups = len(POOL_WINDOWS)
    for r0 in range(0, tm, sub):
        rows = slice(r0, r0 + sub)
        fa = _dot(y_ref[0, 0, rows, :], wa_ref[0]) + _dot(y_ref[0, 1, rows, :], wa_ref[1])
        x = _layer_norm(DEEPNORM_ALPHA * x_ref[0, rows, :] + fa, ga_ref[...], ba_ref[...])
        xb = x.astype(BF16)
        t = s * tm + r0 + lax.broadcasted_iota(jnp.int32, (sub, LANES), 0)

        def project(g):
            cols = slice(g * POOL_GROUP, (g + 1) * POOL_GROUP)
            gcols = slice(POOL_WIDTH + g * POOL_GROUP, POOL_WIDTH + (g + 1) * POOL_GROUP)
            return _dot(xb, win_ref[:, cols]), _dot(xb, win_ref[:, gcols])

        def mix(g, u, gate):
            w = POOL_WINDOWS[g]
            cols = slice(g * POOL_GROUP, (g + 1) * POOL_GROUP)
            ubuf_ref[POOL_HALO + r0:POOL_HALO + r0 + sub, cols] = u
            win = ubuf_ref[r0:r0 + POOL_HALO + sub, cols]
            span = 1
            while span < w:
                win = win + pltpu.roll(win, span, 0)
                span *= 2
            inv = 1.0 / jnp.minimum(t + 1, w).astype(F32)
            inv = jnp.concatenate([inv] * (POOL_GROUP // LANES), axis=-1)
            pooled = win[POOL_HALO:, :] * inv - u
            return _dot((pooled * _silu(gate)).astype(BF16), wout_ref[cols, :])

        ahead = project(0)
        f = jnp.zeros((sub, D_MODEL), F32)
        for g in range(n_groups):
            cur = ahead
            if g + 1 < n_groups:
                ahead = project(g + 1)
            f = f + mix(g, *cur)
        z = DEEPNORM_ALPHA * x + f
        o_ref[0, r0:r0 + sub, :] = _layer_norm(z, g_ref[...], b_ref[...])


def _pool_layer(y, x, w_attn_out, w_in, w_out, ln_g, ln_b, pool_layer, layer, *, tm=512, sub=256):
    B, S, D = x.shape
    full = lambda shape: pl.BlockSpec(shape, lambda b, s: (0,) * len(shape), pipeline_mode=pl.Buffered(1))
    tile = pl.BlockSpec((1, tm, D), lambda b, s: (b, s, 0))
    return pl.pallas_call(
        functools.partial(_pool_kernel, tm=tm, sub=sub),
        grid=(B, S // tm),
        in_specs=[pl.BlockSpec((1, 2, tm, HEAD_WIDTH), lambda b, s: (b, 0, s, 0)), tile,
                  _layer_slab(w_attn_out, pool_layer), _layer_slab(ln_g, layer - 1), _layer_slab(ln_b, layer - 1),
                  full(w_in.shape), _layer_slab(w_out, pool_layer),
                  _layer_slab(ln_g, layer), _layer_slab(ln_b, layer)],
        out_specs=tile,
        out_shape=jax.ShapeDtypeStruct((B, S, D), F32),
        scratch_shapes=[pltpu.VMEM((POOL_HALO + tm, POOL_WIDTH), F32)],
        compiler_params=pltpu.CompilerParams(
            dimension_semantics=("arbitrary", "arbitrary"), vmem_limit_bytes=VMEM_LIMIT),
        name="pool_layer",
    )(y, x, w_attn_out, ln_g, ln_b, w_in, w_out, ln_g, ln_b)


def _split2(v):
    hi = v.astype(BF16)
    return hi, (v - hi.astype(F32)).astype(BF16)


def _pool_fold_kernel(win_ref, wgrp_ref, scale_ref, o_ref):
    i = pl.program_id(0)
    n_groups = len(POOL_WINDOWS)

    @pl.when(i < n_groups)
    def _():
        a_hi, a_lo = _split2(win_ref[...])
        b_hi, b_lo = _split2(wgrp_ref[0])
        prod = (_dot(a_hi, b_hi) + _dot(a_hi, b_lo)) + _dot(a_lo, b_hi)
        o_ref[...] = (prod * scale_ref[...]).astype(BF16)

    @pl.when(i >= n_groups)
    def _():
        o_ref[...] = win_ref[...].astype(BF16)


def _pool_fold_weights(w_in, w_grp, scale, pool_layer):
    n_groups = len(POOL_WINDOWS)
    _, D, width = w_in.shape
    group = lambda i: jnp.minimum(i, n_groups - 1)
    return pl.pallas_call(
        _pool_fold_kernel,
        grid=(width // POOL_GROUP,),
        in_specs=[pl.BlockSpec((None, D, POOL_GROUP), lambda i: (pool_layer, 0, i)),
                  pl.BlockSpec((None, 1, POOL_GROUP, POOL_GROUP), lambda i: (pool_layer, group(i), 0, 0)),
                  pl.BlockSpec((None, 1, POOL_GROUP), lambda i: (pool_layer, 0, group(i)))],
        out_specs=pl.BlockSpec((D, POOL_GROUP), lambda i: (0, i)),
        out_shape=jax.ShapeDtypeStruct((D, width), BF16),
        compiler_params=pltpu.CompilerParams(dimension_semantics=("arbitrary",), vmem_limit_bytes=VMEM_LIMIT),
        name="pool_fold_weights",
    )(w_in, w_grp, scale)


def kernel(x, attn_w_in, attn_b_f, attn_w_out, pool_w_in, pool_w_grp, pool_scale, pool_w_out, ln_g, ln_b):
    W = HEAD_WIDTH
    n_ff = N_HEADS
    w_main = jnp.concatenate([attn_w_in[..., :W] * Q_SCALE, attn_w_in[..., W:4 * W],
                              attn_w_in[..., 4 * W + n_ff:5 * W + n_ff] * Q_SCALE,
                              attn_w_in[..., 5 * W + n_ff:]], axis=-1).astype(BF16)
    w_ff = jnp.pad(attn_w_in[..., 4 * W:4 * W + n_ff], ((0, 0), (0, 0), (0, LANES - n_ff))).astype(BF16)
    b_pad = jnp.pad(attn_b_f, ((0, 0), (0, LANES - n_ff)))[:, None, :]
    w_attn_out = attn_w_out.reshape(-1, 2, W, D_MODEL).astype(BF16)
    w_pool_out = pool_w_out.astype(BF16)
    scale = pool_scale[:, None, :]
    ln_g = ln_g[:, None, :]
    ln_b = ln_b[:, None, :]

    assert DEPTH % 2 == 0
    for layer in range(0, DEPTH, 2):
        j = layer // 2
        y = _attn_core(*_attn_in_proj(x, w_main, w_ff, b_pad, j))
        w_in = _pool_fold_weights(pool_w_in, pool_w_grp, scale, j)
        x = _pool_layer(y, x, w_attn_out, w_in, w_pool_out, ln_g, ln_b, j, layer + 1)
    return x
```

```python
import functools
import math

import jax
import jax.numpy as jnp
import numpy as np
from jax import lax
from jax.experimental import pallas as pl
from jax.experimental.pallas import tpu as pltpu

D_MODEL = 1024
DEPTH = 4
HEAD_DIM = 64
N_HEADS = 8
HEAD_WIDTH = N_HEADS * HEAD_DIM
N_PAIRS = N_HEADS // 2
MOBA_BLOCK = 256
MOBA_TOPK = 3
POOL_WIDTH = 2048
POOL_WINDOWS = (2, 4, 8, 16)
POOL_GROUP = 512
POOL_HALO = 16
DEEPNORM_ALPHA = (2 * DEPTH) ** 0.25
LN_EPS = 1e-5
LOG2E = math.log2(math.e)
Q_SCALE = HEAD_DIM ** -0.5 * LOG2E

LANES = 128
SUBLANES = 8
BF16_SUBLANES = 2 * SUBLANES
V_ROWS = HEAD_DIM + BF16_SUBLANES
MASK_BIAS = -30000.0
VMEM_LIMIT = 56 * 1024 * 1024

BF16 = jnp.bfloat16
F32 = jnp.float32


def _dot(a, b):
    return jnp.dot(a, b, preferred_element_type=F32)


def _split3(v):
    hi = v.astype(BF16)
    r1 = v - hi.astype(F32)
    mid = r1.astype(BF16)
    lo = (r1 - mid.astype(F32)).astype(BF16)
    return hi, mid, lo


def _layer_norm(z, g, b):
    mu = jnp.mean(z, axis=-1, keepdims=True)
    zc = z - mu
    var = jnp.mean(zc * zc, axis=-1, keepdims=True)
    return zc * lax.rsqrt(var + LN_EPS) * g + b


def _silu(v):
    return v * jax.nn.sigmoid(v)


def _pack3(v):
    lane = lax.broadcasted_iota(jnp.int32, v.shape, 1)
    hi, mid, lo = (part.astype(F32) for part in _split3(v))
    packed = jnp.where(lane < N_HEADS, hi,
                       jnp.where(lane < 2 * N_HEADS, pltpu.roll(mid, N_HEADS, 1),
                                 jnp.where(lane < 3 * N_HEADS, pltpu.roll(lo, 2 * N_HEADS, 1), 0.0)))
    return packed.astype(BF16)


def _attn_in_kernel(x_ref, w_ref, wff_ref, bf_ref, place_ref, const_ref,
                    fq_ref, fk_ref, fv_ref, fg_ref, mq_ref, mk_ref, mv_ref, mg_ref,
                    carry_ref, *, tm, sub):
    s = pl.program_id(1)

    @pl.when(s == 0)
    def _():
        carry_ref[...] = jnp.zeros_like(carry_ref)

    weights = (w_ref, wff_ref, bf_ref, place_ref, const_ref)
    outs = (fq_ref, fk_ref, fv_ref, fg_ref, mq_ref, mk_ref, mv_ref, mg_ref)
    for r0 in range(0, tm, sub):
        _attn_in_subtile(x_ref, weights, outs, carry_ref, s * tm + r0, slice(r0, r0 + sub))


def _attn_in_subtile(x_ref, weights, outs, carry_ref, t0, rows):
    w_ref, wff_ref, bf_ref, place_ref, const_ref = weights
    fq_ref, fk_ref, fv_ref, fg_ref, mq_ref, mk_ref, mv_ref, mg_ref = outs
    tm = rows.stop - rows.start
    xb = x_ref[0, rows, :].astype(BF16)
    W = HEAD_WIDTH

    def project(i):
        return _dot(xb, w_ref[:, i * W:(i + 1) * W])

    blk = (t0 + lax.broadcasted_iota(jnp.int32, (tm, LANES), 0)) // MOBA_BLOCK
    onehot = jnp.where(lax.broadcasted_iota(jnp.int32, (tm, LANES), 1) == blk, 1.0, 0.0).astype(BF16)

    def pair_lanes(p):
        return slice(2 * p * LANES, (2 * p + 1) * LANES), slice((2 * p + 1) * LANES, (2 * p + 2) * LANES)

    def store_values(v_ref, h):
        ones = jnp.ones((V_ROWS - HEAD_DIM, tm), BF16)
        for p in range(N_PAIRS):
            vt = h[:, p * LANES:(p + 1) * LANES].T.astype(BF16)
            for a in range(2):
                h0 = (2 * p + a) * V_ROWS
                v_ref[0, h0:h0 + HEAD_DIM, rows] = vt[a * HEAD_DIM:(a + 1) * HEAD_DIM, :]
                v_ref[0, h0 + HEAD_DIM:h0 + V_ROWS, rows] = ones

    head_lanes = lax.broadcasted_iota(jnp.int32, wff_ref.shape, 1) < N_HEADS
    w_ff = jnp.where(head_lanes, wff_ref[...], 0.0).astype(BF16)
    ff = _dot(xb, w_ff) + bf_ref[...]
    log_f_parts = _pack3(jax.nn.log_sigmoid(ff))
    store_values(fv_ref, project(2))
    fg_ref[0, rows, :] = _silu(project(3)).astype(BF16)

    row = lax.broadcasted_iota(jnp.int32, (tm, tm), 0)
    col = lax.broadcasted_iota(jnp.int32, (tm, tm), 1)
    tri = jnp.where(col <= row, 1.0, 0.0).astype(BF16)
    parts = _dot(tri, log_f_parts)
    c = (parts + pltpu.roll(parts, LANES - N_HEADS, 1)) + pltpu.roll(parts, LANES - 2 * N_HEADS, 1)
    c = c + carry_ref[0:1, :]
    carry_ref[0:1, :] = c[tm - 1:tm, :]
    c_parts = _pack3(c * LOG2E)
    h = project(4)
    for p in range(N_PAIRS):
        mq_ref[0, p * LANES:(p + 1) * LANES, rows] = h[:, p * LANES:(p + 1) * LANES].T.astype(BF16)
    h = project(5)
    for p in range(N_PAIRS):
        lo_lanes, hi_lanes = pair_lanes(p)
        mk_ref[0, rows, lo_lanes] = h[:, p * LANES:(p + 1) * LANES].astype(BF16)
        mk_ref[0, rows, hi_lanes] = onehot
    ext = _dot(c_parts, place_ref[...]) + const_ref[...]

    store_values(mv_ref, project(6))
    mg_ref[0, rows, :] = _silu(project(7)).astype(BF16)
    h = project(0)
    for p in range(N_PAIRS):
        lo_lanes, hi_lanes = pair_lanes(p)
        fq_ref[0, lo_lanes, rows] = h[:, p * LANES:(p + 1) * LANES].T.astype(BF16)
        fq_ref[0, hi_lanes, rows] = ext[:, p * LANES:(p + 1) * LANES].T.astype(BF16)
    h = project(1)
    for p in range(N_PAIRS):
        lo_lanes, hi_lanes = pair_lanes(p)
        fk_ref[0, rows, lo_lanes] = h[:, p * LANES:(p + 1) * LANES].astype(BF16)
        fk_ref[0, rows, hi_lanes] = ext[:, W + p * LANES:W + (p + 1) * LANES].astype(BF16)


def _decay_placement():
    place = np.zeros((LANES, 2 * HEAD_WIDTH), np.float32)
    const = np.zeros((1, 2 * HEAD_WIDTH), np.float32)
    for h in range(N_HEADS):
        p, a = divmod(h, 2)
        base_q = p * LANES + a * HEAD_DIM
        base_k = HEAD_WIDTH + p * LANES + a * HEAD_DIM
        for r in range(3):
            place[r * N_HEADS + h, base_q + r] = 1.0
            const[0, base_q + 3 + r] = 1.0
            const[0, base_k + r] = 1.0
            place[r * N_HEADS + h, base_k + 3 + r] = -1.0
    return jnp.asarray(place, BF16), jnp.asarray(const, F32)


def _layer_slab(stacked, layer):
    tail = stacked.shape[1:]
    return pl.BlockSpec((None,) + tail, lambda *_: (layer,) + (0,) * len(tail), pipeline_mode=pl.Buffered(1))


def _attn_in_proj(x, w_main, w_in_raw, b_f, layer, *, tm=512, sub=256):
    B, S, D = x.shape
    place, const = _decay_placement()
    W = HEAD_WIDTH
    ff_block = pl.BlockSpec((None, D, LANES), lambda b, s: (layer, 0, 4 * W // LANES),
                            pipeline_mode=pl.Buffered(1))
    full = lambda shape: pl.BlockSpec(shape, lambda b, s: (0,) * len(shape))
    tile = lambda width: pl.BlockSpec((1, tm, width), lambda b, s: (b, s, 0))
    row_major = lambda w: (tile(w), jax.ShapeDtypeStruct((B, S, w), BF16))
    transposed = lambda rows: (pl.BlockSpec((1, rows, tm), lambda b, s: (b, 0, s)),
                               jax.ShapeDtypeStruct((B, rows, S), BF16))
    values = transposed(N_HEADS * V_ROWS)
    outs = (transposed(2 * W), row_major(2 * W), values, row_major(W),
            transposed(W), row_major(2 * W), values, row_major(W))
    return pl.pallas_call(
        functools.partial(_attn_in_kernel, tm=tm, sub=sub),
        grid=(B, S // tm),
        in_specs=[tile(D), _layer_slab(w_main, layer), ff_block, _layer_slab(b_f, layer),
                  full(place.shape), full(const.shape)],
        out_specs=[o[0] for o in outs],
        out_shape=[o[1] for o in outs],
        scratch_shapes=[pltpu.VMEM((8, LANES), F32)],
        compiler_params=pltpu.CompilerParams(
            dimension_semantics=("arbitrary", "arbitrary"), vmem_limit_bytes=VMEM_LIMIT),
        name="attn_in_proj",
    )(x, w_main, w_in_raw, b_f, place, const)


def _head_row_mask(shape, a):
    row = lax.broadcasted_iota(jnp.int32, shape, 0)
    return (row // HEAD_DIM) % 2 == a


def _moba_bias_t(qa_t, kmean, qi, nb):
    tq = qa_t.shape[1]
    gate = _dot(kmean, qa_t)[0:SUBLANES, :]
    blk = lax.broadcasted_iota(jnp.int32, gate.shape, 0)
    gate = jnp.where(blk < qi, gate, -jnp.inf)
    rank = jnp.zeros(gate.shape, jnp.int32)
    for d in range(1, nb):
        other = pltpu.roll(gate, d, 0)
        rank = rank + jnp.where(blk >= d, jnp.where(other >= gate, 1, 0), jnp.where(other > gate, 1, 0))
    keep = ((rank < MOBA_TOPK) & (gate > -jnp.inf)) | (blk == qi)
    bias_t = jnp.where(keep, 0.0, MASK_BIAS)
    return jnp.concatenate([bias_t, jnp.zeros((LANES - SUBLANES, tq), F32)], axis=0).astype(BF16)


def _attn_core_kernel(fq_ref, fk_ref, fv_ref, fg_ref, mq_ref, mk_ref, mv_ref, mg_ref, y_ref,
                      kmean_ref, qt_ref, s_ref, acc_ref, *, seq):
    tq = MOBA_BLOCK
    nb = seq // tq
    assert nb == SUBLANES
    qi = pl.program_id(1)

    @pl.when(qi == 0)
    def _():
        r = lax.broadcasted_iota(jnp.int32, (LANES, seq), 0)
        t = lax.broadcasted_iota(jnp.int32, (LANES, seq), 1)
        avg = jnp.where(t // tq == r, 1.0 / tq, 0.0).astype(BF16)
        for p in range(N_PAIRS):
            kmean_ref[p] = _dot(avg, mk_ref[0, :, 2 * p * LANES:(2 * p + 1) * LANES])

    chains = []
    for p in range(N_PAIRS):
        fq_t = fq_ref[0, 2 * p * LANES:(2 * p + 2) * LANES, :]
        for a in range(2):
            qt_ref[len(chains)] = jnp.where(_head_row_mask(fq_t.shape, a), fq_t, jnp.zeros_like(fq_t))
            chains.append((fk_ref, 2 * p * LANES, fv_ref, (2 * p + a) * V_ROWS))
    for p in range(N_PAIRS):
        mq_t = mq_ref[0, p * LANES:(p + 1) * LANES, :]
        kmean = kmean_ref[p].astype(BF16)
        for a in range(2):
            qa_t = jnp.where(_head_row_mask(mq_t.shape, a), mq_t, jnp.zeros_like(mq_t))
            qt_ref[len(chains)] = jnp.concatenate([qa_t, _moba_bias_t(qa_t, kmean, qi, nb)], axis=0)
            chains.append((mk_ref, 2 * p * LANES, mv_ref, (2 * p + a) * V_ROWS))
    n = len(chains)

    def tile_step(start, ms, nk=1, mask=None, first=False):
        keys = nk * tq
        tile_max = []
        for c, (k_ref, k_off, _, _) in enumerate(chains):
            s_t = _dot(k_ref[0, pl.ds(start, keys), k_off:k_off + 2 * LANES], qt_ref[c])
            if mask is not None:
                s_t = jnp.where(mask, s_t, -jnp.inf)
            s_ref[c, 0:keys, :] = s_t
            tile_max.append(jnp.max(s_t, axis=0, keepdims=True))
        new_ms = []
        for c, (m, (_, _, v_ref, v_row)) in enumerate(zip(ms, chains)):
            m_new = jnp.maximum(m, tile_max[c])
            new_ms.append(m_new)
            p_t = jnp.exp2(s_ref[c, 0:keys, :] - m_new).astype(BF16)
            pv = _dot(v_ref[0, v_row:v_row + V_ROWS, pl.ds(start, keys)], p_t)
            acc_ref[c] = pv if first else jnp.exp2(m - m_new) * acc_ref[c] + pv
        return new_ms

    key = lax.broadcasted_iota(jnp.int32, (tq, tq), 0)
    qry = lax.broadcasted_iota(jnp.int32, (tq, tq), 1)
    ms = tile_step(pl.multiple_of(qi * tq, tq), [jnp.full((1, tq), -jnp.inf, F32)] * n,
                   mask=key <= qry, first=True)
    ms = lax.fori_loop(0, qi // 2, lambda j, ms: tile_step(pl.multiple_of(j * 2 * tq, 2 * tq), ms, nk=2), ms)

    @pl.when(qi % 2 == 1)
    def _():
        tile_step(pl.multiple_of((qi - 1) * tq, tq), ms)

    for kind, g_ref in enumerate((fg_ref, mg_ref)):
        for p in range(N_PAIRS):
            o = []
            for c in (kind * N_HEADS + 2 * p, kind * N_HEADS + 2 * p + 1):
                o.append(acc_ref[c, 0:HEAD_DIM, :] / acc_ref[c, HEAD_DIM:HEAD_DIM + 1, :])
            y = jnp.concatenate(o, axis=0).T
            lanes = slice(p * LANES, (p + 1) * LANES)
            y_ref[0, kind, :, lanes] = (y * g_ref[0, :, lanes].astype(F32)).astype(BF16)


def _attn_core(fq, fk, fv, fg, mq, mk, mv, mg):
    B, S, _ = fk.shape
    tq = MOBA_BLOCK
    n_chains = 2 * N_HEADS
    q_tile = lambda width: pl.BlockSpec((1, tq, width), lambda b, i: (b, i, 0))
    qt_tile = lambda rows: pl.BlockSpec((1, rows, tq), lambda b, i: (b, 0, i))
    k_full = pl.BlockSpec((1, S, 2 * HEAD_WIDTH), lambda b, i: (b, 0, 0))
    v_full = pl.BlockSpec((1, N_HEADS * V_ROWS, S), lambda b, i: (b, 0, 0))
    return pl.pallas_call(
        functools.partial(_attn_core_kernel, seq=S),
        grid=(B, S // tq),
        in_specs=[qt_tile(2 * HEAD_WIDTH), k_full, v_full, q_tile(HEAD_WIDTH),
                  qt_tile(HEAD_WIDTH), k_full, v_full, q_tile(HEAD_WIDTH)],
        out_specs=pl.BlockSpec((1, 2, tq, HEAD_WIDTH), lambda b, i: (b, 0, i, 0)),
        out_shape=jax.ShapeDtypeStruct((B, 2, S, HEAD_WIDTH), BF16),
        scratch_shapes=[pltpu.VMEM((N_PAIRS, LANES, LANES), F32),
                        pltpu.VMEM((n_chains, 2 * LANES, tq), BF16),
                        pltpu.VMEM((n_chains, 2 * tq, tq), F32),
                        pltpu.VMEM((n_chains, V_ROWS, tq), F32)],
        compiler_params=pltpu.CompilerParams(
            dimension_semantics=("arbitrary", "arbitrary"), vmem_limit_bytes=VMEM_LIMIT),
        name="attn_core",
    )(fq, fk, fv, fg, mq, mk, mv, mg)


def _pool_kernel(y_ref, x_ref, wa_ref, ga_ref, ba_ref, win_ref, wout_ref, g_ref, b_ref, o_ref, ubuf_ref,
                 *, tm, sub):
    s = pl.program_id(1)

    @pl.when(s == 0)
    def _():
        ubuf_ref[0:POOL_HALO, :] = jnp.zeros((POOL_HALO, POOL_WIDTH), F32)

    @pl.when(s > 0)
    def _():
        ubuf_ref[0:POOL_HALO, :] = ubuf_ref[tm:tm + POOL_HALO, :]

    n_groups = len(POOL_WINDOWS)

    def attn_out(r0):
        rows = slice(r0, r0 + sub)
        fa = _dot(y_ref[0, 0, rows, :], wa_ref[0]) + _dot(y_ref[0, 1, rows, :], wa_ref[1])
        return _layer_norm(DEEPNORM_ALPHA * x_ref[0, rows, :] + fa, ga_ref[...], ba_ref[...])

    x_next = attn_out(0)
    for r0 in range(0, tm, sub):
        x = x_next
        if r0 + sub < tm:
            x_next = attn_out(r0 + sub)
        xb = x.astype(BF16)
        t = s * tm + r0 + lax.broadcasted_iota(jnp.int32, (sub, LANES), 0)

        def project(g):
            cols = slice(g * POOL_GROUP, (g + 1) * POOL_GROUP)
            gcols = slice(POOL_WIDTH + g * POOL_GROUP, POOL_WIDTH + (g + 1) * POOL_GROUP)
            return _dot(xb, win_ref[:, cols]), _dot(xb, win_ref[:, gcols])

        def mix(g, u, gate):
            w = POOL_WINDOWS[g]
            cols = slice(g * POOL_GROUP, (g + 1) * POOL_GROUP)
            ubuf_ref[POOL_HALO + r0:POOL_HALO + r0 + sub, cols] = u
            win = ubuf_ref[r0:r0 + POOL_HALO + sub, cols]
            span = 1
            while span < w:
                win = win + pltpu.roll(win, span, 0)
                span *= 2
            inv = 1.0 / jnp.minimum(t + 1, w).astype(F32)
            inv = jnp.concatenate([inv] * (POOL_GROUP // LANES), axis=-1)
            pooled = win[POOL_HALO:, :] * inv - u
            return _dot((pooled * _silu(gate)).astype(BF16), wout_ref[cols, :])

        ahead = project(0)
        f = jnp.zeros((sub, D_MODEL), F32)
        for g in range(n_groups):
            cur = ahead
            if g + 1 < n_groups:
                ahead = project(g + 1)
            f = f + mix(g, *cur)
        z = DEEPNORM_ALPHA * x + f
        o_ref[0, r0:r0 + sub, :] = _layer_norm(z, g_ref[...], b_ref[...])


def _pool_layer(y, x, w_attn_out, w_in, w_out, ln_g, ln_b, pool_layer, layer, *, tm=512, sub=256):
    B, S, D = x.shape
    full = lambda shape: pl.BlockSpec(shape, lambda b, s: (0,) * len(shape), pipeline_mode=pl.Buffered(1))
    tile = pl.BlockSpec((1, tm, D), lambda b, s: (b, s, 0))
    return pl.pallas_call(
        functools.partial(_pool_kernel, tm=tm, sub=sub),
        grid=(B, S // tm),
        in_specs=[pl.BlockSpec((1, 2, tm, HEAD_WIDTH), lambda b, s: (b, 0, s, 0)), tile,
                  _layer_slab(w_attn_out, pool_layer), _layer_slab(ln_g, layer - 1), _layer_slab(ln_b, layer - 1),
                  full(w_in.shape), _layer_slab(w_out, pool_layer),
                  _layer_slab(ln_g, layer), _layer_slab(ln_b, layer)],
        out_specs=tile,
        out_shape=jax.ShapeDtypeStruct((B, S, D), F32),
        scratch_shapes=[pltpu.VMEM((POOL_HALO + tm, POOL_WIDTH), F32)],
        compiler_params=pltpu.CompilerParams(
            dimension_semantics=("arbitrary", "arbitrary"), vmem_limit_bytes=VMEM_LIMIT),
        name="pool_layer",
    )(y, x, w_attn_out, ln_g, ln_b, w_in, w_out, ln_g, ln_b)


def _split2(v):
    hi = v.astype(BF16)
    return hi, (v - hi.astype(F32)).astype(BF16)


def _pool_fold_kernel(win_ref, wgrp_ref, scale_ref, o_ref):
    i = pl.program_id(0)
    n_groups = len(POOL_WINDOWS)

    @pl.when(i < n_groups)
    def _():
        a_hi, a_lo = _split2(win_ref[...])
        b_hi, b_lo = _split2(wgrp_ref[0])
        prod = (_dot(a_hi, b_hi) + _dot(a_hi, b_lo)) + _dot(a_lo, b_hi)
        o_ref[...] = (prod * scale_ref[...]).astype(BF16)

    @pl.when(i >= n_groups)
    def _():
        o_ref[...] = win_ref[...].astype(BF16)


def _pool_fold_weights(w_in, w_grp, scale, pool_layer):
    n_groups = len(POOL_WINDOWS)
    _, D, width = w_in.shape
    group = lambda i: jnp.minimum(i, n_groups - 1)
    return pl.pallas_call(
        _pool_fold_kernel,
        grid=(width // POOL_GROUP,),
        in_specs=[pl.BlockSpec((None, D, POOL_GROUP), lambda i: (pool_layer, 0, i)),
                  pl.BlockSpec((None, 1, POOL_GROUP, POOL_GROUP), lambda i: (pool_layer, group(i), 0, 0)),
                  pl.BlockSpec((None, 1, POOL_GROUP), lambda i: (pool_layer, 0, group(i)))],
        out_specs=pl.BlockSpec((D, POOL_GROUP), lambda i: (0, i)),
        out_shape=jax.ShapeDtypeStruct((D, width), BF16),
        compiler_params=pltpu.CompilerParams(dimension_semantics=("arbitrary",), vmem_limit_bytes=VMEM_LIMIT),
        name="pool_fold_weights",
    )(w_in, w_grp, scale)


def kernel(x, attn_w_in, attn_b_f, attn_w_out, pool_w_in, pool_w_grp, pool_scale, pool_w_out, ln_g, ln_b):
    W = HEAD_WIDTH
    n_ff = N_HEADS
    w_main = jnp.concatenate([attn_w_in[..., :W] * Q_SCALE, attn_w_in[..., W:4 * W],
                              attn_w_in[..., 4 * W + n_ff:5 * W + n_ff] * Q_SCALE,
                              attn_w_in[..., 5 * W + n_ff:]], axis=-1).astype(BF16)
    b_pad = jnp.pad(attn_b_f, ((0, 0), (0, LANES - n_ff)))[:, None, :]
    w_attn_out = attn_w_out.reshape(-1, 2, W, D_MODEL).astype(BF16)
    w_pool_out = pool_w_out.astype(BF16)
    scale = pool_scale[:, None, :]
    ln_g = ln_g[:, None, :]
    ln_b = ln_b[:, None, :]

    assert DEPTH % 2 == 0
    for layer in range(0, DEPTH, 2):
        j = layer // 2
        y = _attn_core(*_attn_in_proj(x, w_main, attn_w_in, b_pad, j))
        w_in = _pool_fold_weights(pool_w_in, pool_w_grp, scale, j)
        x = _pool_layer(y, x, w_attn_out, w_in, w_pool_out, ln_g, ln_b, j, layer + 1)
    return x
```

```python
import functools
import math

import jax
import jax.numpy as jnp
import numpy as np
from jax import lax
from jax.experimental import pallas as pl
from jax.experimental.pallas import tpu as pltpu

D_MODEL = 1024
DEPTH = 4
HEAD_DIM = 64
N_HEADS = 8
HEAD_WIDTH = N_HEADS * HEAD_DIM
N_PAIRS = N_HEADS // 2
MOBA_BLOCK = 256
MOBA_TOPK = 3
POOL_WIDTH = 2048
POOL_WINDOWS = (2, 4, 8, 16)
POOL_GROUP = 512
POOL_HALO = 16
DEEPNORM_ALPHA = (2 * DEPTH) ** 0.25
LN_EPS = 1e-5
LOG2E = math.log2(math.e)
Q_SCALE = HEAD_DIM ** -0.5 * LOG2E

LANES = 128
SUBLANES = 8
BF16_SUBLANES = 2 * SUBLANES
V_ROWS = HEAD_DIM + BF16_SUBLANES
MASK_BIAS = -30000.0
VMEM_LIMIT = 56 * 1024 * 1024

BF16 = jnp.bfloat16
F32 = jnp.float32


def _dot(a, b):
    return jnp.dot(a, b, preferred_element_type=F32)


def _split3(v):
    hi = v.astype(BF16)
    r1 = v - hi.astype(F32)
    mid = r1.astype(BF16)
    lo = (r1 - mid.astype(F32)).astype(BF16)
    return hi, mid, lo


def _layer_norm(z, g, b):
    mu = jnp.mean(z, axis=-1, keepdims=True)
    zc = z - mu
    var = jnp.mean(zc * zc, axis=-1, keepdims=True)
    return zc * lax.rsqrt(var + LN_EPS) * g + b


def _silu(v):
    return v * jax.nn.sigmoid(v)


def _pack3(v):
    lane = lax.broadcasted_iota(jnp.int32, v.shape, 1)
    hi, mid, lo = (part.astype(F32) for part in _split3(v))
    packed = jnp.where(lane < N_HEADS, hi,
                       jnp.where(lane < 2 * N_HEADS, pltpu.roll(mid, N_HEADS, 1),
                                 jnp.where(lane < 3 * N_HEADS, pltpu.roll(lo, 2 * N_HEADS, 1), 0.0)))
    return packed.astype(BF16)


def _attn_in_kernel(x_ref, w_ref, wff_ref, bf_ref, place_ref, const_ref,
                    fq_ref, fk_ref, fv_ref, fg_ref, mq_ref, mk_ref, mv_ref, mg_ref,
                    carry_ref, *, tm, sub):
    s = pl.program_id(1)

    @pl.when(s == 0)
    def _():
        carry_ref[...] = jnp.zeros_like(carry_ref)

    weights = (w_ref, wff_ref, bf_ref, place_ref, const_ref)
    outs = (fq_ref, fk_ref, fv_ref, fg_ref, mq_ref, mk_ref, mv_ref, mg_ref)
    for r0 in range(0, tm, sub):
        _attn_in_subtile(x_ref, weights, outs, carry_ref, s * tm + r0, slice(r0, r0 + sub))


def _attn_in_subtile(x_ref, weights, outs, carry_ref, t0, rows):
    w_ref, wff_ref, bf_ref, place_ref, const_ref = weights
    fq_ref, fk_ref, fv_ref, fg_ref, mq_ref, mk_ref, mv_ref, mg_ref = outs
    tm = rows.stop - rows.start
    xb = x_ref[0, rows, :].astype(BF16)
    W = HEAD_WIDTH

    def project(i):
        return _dot(xb, w_ref[:, i * W:(i + 1) * W])

    blk = (t0 + lax.broadcasted_iota(jnp.int32, (tm, LANES), 0)) // MOBA_BLOCK
    onehot = jnp.where(lax.broadcasted_iota(jnp.int32, (tm, LANES), 1) == blk, 1.0, 0.0).astype(BF16)

    def pair_lanes(p):
        return slice(2 * p * LANES, (2 * p + 1) * LANES), slice((2 * p + 1) * LANES, (2 * p + 2) * LANES)

    def store_values(v_ref, h):
        ones = jnp.ones((V_ROWS - HEAD_DIM, tm), BF16)
        for p in range(N_PAIRS):
            vt = h[:, p * LANES:(p + 1) * LANES].T.astype(BF16)
            for a in range(2):
                h0 = (2 * p + a) * V_ROWS
                v_ref[0, h0:h0 + HEAD_DIM, rows] = vt[a * HEAD_DIM:(a + 1) * HEAD_DIM, :]
                v_ref[0, h0 + HEAD_DIM:h0 + V_ROWS, rows] = ones

    head_lanes = lax.broadcasted_iota(jnp.int32, wff_ref.shape, 1) < N_HEADS
    w_ff = jnp.where(head_lanes, wff_ref[...], 0.0).astype(BF16)
    ff = _dot(xb, w_ff) + bf_ref[...]
    log_f_parts = _pack3(jax.nn.log_sigmoid(ff))
    store_values(fv_ref, project(2))
    fg_ref[0, rows, :] = _silu(project(3)).astype(BF16)

    row = lax.broadcasted_iota(jnp.int32, (tm, tm), 0)
    col = lax.broadcasted_iota(jnp.int32, (tm, tm), 1)
    tri = jnp.where(col <= row, 1.0, 0.0).astype(BF16)
    parts = _dot(tri, log_f_parts)
    c = (parts + pltpu.roll(parts, LANES - N_HEADS, 1)) + pltpu.roll(parts, LANES - 2 * N_HEADS, 1)
    c = c + carry_ref[0:1, :]
    carry_ref[0:1, :] = c[tm - 1:tm, :]
    c_parts = _pack3(c * LOG2E)
    h = project(4)
    for p in range(N_PAIRS):
        mq_ref[0, p * LANES:(p + 1) * LANES, rows] = h[:, p * LANES:(p + 1) * LANES].T.astype(BF16)
    h = project(5)
    for p in range(N_PAIRS):
        lo_lanes, hi_lanes = pair_lanes(p)
        mk_ref[0, rows, lo_lanes] = h[:, p * LANES:(p + 1) * LANES].astype(BF16)
        mk_ref[0, rows, hi_lanes] = onehot
    ext = _dot(c_parts, place_ref[...]) + const_ref[...]

    store_values(mv_ref, project(6))
    mg_ref[0, rows, :] = _silu(project(7)).astype(BF16)
    h = project(0)
    for p in range(N_PAIRS):
        lo_lanes, hi_lanes = pair_lanes(p)
        fq_ref[0, lo_lanes, rows] = h[:, p * LANES:(p + 1) * LANES].T.astype(BF16)
        fq_ref[0, hi_lanes, rows] = ext[:, p * LANES:(p + 1) * LANES].T.astype(BF16)
    h = project(1)
    for p in range(N_PAIRS):
        lo_lanes, hi_lanes = pair_lanes(p)
        fk_ref[0, rows, lo_lanes] = h[:, p * LANES:(p + 1) * LANES].astype(BF16)
        fk_ref[0, rows, hi_lanes] = ext[:, W + p * LANES:W + (p + 1) * LANES].astype(BF16)


def _decay_placement():
    place = np.zeros((LANES, 2 * HEAD_WIDTH), np.float32)
    const = np.zeros((1, 2 * HEAD_WIDTH), np.float32)
    for h in range(N_HEADS):
        p, a = divmod(h, 2)
        base_q = p * LANES + a * HEAD_DIM
        base_k = HEAD_WIDTH + p * LANES + a * HEAD_DIM
        for r in range(3):
            place[r * N_HEADS + h, base_q + r] = 1.0
            const[0, base_q + 3 + r] = 1.0
            const[0, base_k + r] = 1.0
            place[r * N_HEADS + h, base_k + 3 + r] = -1.0
    return jnp.asarray(place, BF16), jnp.asarray(const, F32)


KIND_GROUPS = 4


def _attn_prep_kernel(a_ref, b_ref, o_ref):
    g = pl.program_id(1)
    scale = jnp.where((g == 0) | (g == KIND_GROUPS), Q_SCALE, 1.0)

    @pl.when(g < KIND_GROUPS)
    def _():
        o_ref[...] = (a_ref[...] * scale).astype(BF16)

    @pl.when(g >= KIND_GROUPS)
    def _():
        width = a_ref.shape[1]
        lane = lax.broadcasted_iota(jnp.int32, a_ref.shape, 1)
        shifted = jnp.where(lane < width - N_HEADS, pltpu.roll(a_ref[...], width - N_HEADS, 1),
                            pltpu.roll(b_ref[...], width - N_HEADS, 1))
        o_ref[...] = (shifted * scale).astype(BF16)


def _attn_prep_weights(w_in):
    L, D, _ = w_in.shape
    W = HEAD_WIDTH
    return pl.pallas_call(
        _attn_prep_kernel,
        grid=(L, 2 * KIND_GROUPS),
        in_specs=[pl.BlockSpec((None, D, W), lambda l, g: (l, 0, g)),
                  pl.BlockSpec((None, D, W), lambda l, g: (l, 0, jnp.where(g >= KIND_GROUPS, g + 1, g)))],
        out_specs=pl.BlockSpec((None, D, W), lambda l, g: (l, 0, g)),
        out_shape=jax.ShapeDtypeStruct((L, D, 2 * KIND_GROUPS * W), BF16),
        compiler_params=pltpu.CompilerParams(
            dimension_semantics=("arbitrary", "arbitrary"), vmem_limit_bytes=VMEM_LIMIT),
        name="attn_prep_weights",
    )(w_in, w_in)


def _layer_slab(stacked, layer):
    tail = stacked.shape[1:]
    return pl.BlockSpec((None,) + tail, lambda *_: (layer,) + (0,) * len(tail), pipeline_mode=pl.Buffered(1))


def _attn_in_proj(x, w_main, w_in_raw, b_f, layer, *, tm=512, sub=256):
    B, S, D = x.shape
    place, const = _decay_placement()
    W = HEAD_WIDTH
    ff_block = pl.BlockSpec((None, D, LANES), lambda b, s: (layer, 0, 4 * W // LANES),
                            pipeline_mode=pl.Buffered(1))
    full = lambda shape: pl.BlockSpec(shape, lambda b, s: (0,) * len(shape))
    tile = lambda width: pl.BlockSpec((1, tm, width), lambda b, s: (b, s, 0))
    row_major = lambda w: (tile(w), jax.ShapeDtypeStruct((B, S, w), BF16))
    transposed = lambda rows: (pl.BlockSpec((1, rows, tm), lambda b, s: (b, 0, s)),
                               jax.ShapeDtypeStruct((B, rows, S), BF16))
    values = transposed(N_HEADS * V_ROWS)
    outs = (transposed(2 * W), row_major(2 * W), values, row_major(W),
            transposed(W), row_major(2 * W), values, row_major(W))
    return pl.pallas_call(
        functools.partial(_attn_in_kernel, tm=tm, sub=sub),
        grid=(B, S // tm),
        in_specs=[tile(D), _layer_slab(w_main, layer), ff_block, _layer_slab(b_f, layer),
                  full(place.shape), full(const.shape)],
        out_specs=[o[0] for o in outs],
        out_shape=[o[1] for o in outs],
        scratch_shapes=[pltpu.VMEM((8, LANES), F32)],
        compiler_params=pltpu.CompilerParams(
            dimension_semantics=("arbitrary", "arbitrary"), vmem_limit_bytes=VMEM_LIMIT),
        name="attn_in_proj",
    )(x, w_main, w_in_raw, b_f, place, const)


def _head_row_mask(shape, a):
    row = lax.broadcasted_iota(jnp.int32, shape, 0)
    return (row // HEAD_DIM) % 2 == a


def _moba_bias_t(qa_t, kmean, qi, nb):
    tq = qa_t.shape[1]
    gate = _dot(kmean, qa_t)[0:SUBLANES, :]
    blk = lax.broadcasted_iota(jnp.int32, gate.shape, 0)
    gate = jnp.where(blk < qi, gate, -jnp.inf)
    rank = jnp.zeros(gate.shape, jnp.int32)
    for d in range(1, nb):
        other = pltpu.roll(gate, d, 0)
        rank = rank + jnp.where(blk >= d, jnp.where(other >= gate, 1, 0), jnp.where(other > gate, 1, 0))
    keep = ((rank < MOBA_TOPK) & (gate > -jnp.inf)) | (blk == qi)
    bias_t = jnp.where(keep, 0.0, MASK_BIAS)
    return jnp.concatenate([bias_t, jnp.zeros((LANES - SUBLANES, tq), F32)], axis=0).astype(BF16)


def _attn_core_kernel(fq_ref, fk_ref, fv_ref, fg_ref, mq_ref, mk_ref, mv_ref, mg_ref, y_ref,
                      kmean_ref, qt_ref, s_ref, acc_ref, *, seq):
    tq = MOBA_BLOCK
    nb = seq // tq
    assert nb == SUBLANES
    qi = pl.program_id(1)

    @pl.when(qi == 0)
    def _():
        r = lax.broadcasted_iota(jnp.int32, (LANES, seq), 0)
        t = lax.broadcasted_iota(jnp.int32, (LANES, seq), 1)
        avg = jnp.where(t // tq == r, 1.0 / tq, 0.0).astype(BF16)
        for p in range(N_PAIRS):
            kmean_ref[p] = _dot(avg, mk_ref[0, :, 2 * p * LANES:(2 * p + 1) * LANES])

    chains = []
    for p in range(N_PAIRS):
        fq_t = fq_ref[0, 2 * p * LANES:(2 * p + 2) * LANES, :]
        for a in range(2):
            qt_ref[len(chains)] = jnp.where(_head_row_mask(fq_t.shape, a), fq_t, jnp.zeros_like(fq_t))
            chains.append((fk_ref, 2 * p * LANES, fv_ref, (2 * p + a) * V_ROWS))
    for p in range(N_PAIRS):
        mq_t = mq_ref[0, p * LANES:(p + 1) * LANES, :]
        kmean = kmean_ref[p].astype(BF16)
        for a in range(2):
            qa_t = jnp.where(_head_row_mask(mq_t.shape, a), mq_t, jnp.zeros_like(mq_t))
            qt_ref[len(chains)] = jnp.concatenate([qa_t, _moba_bias_t(qa_t, kmean, qi, nb)], axis=0)
            chains.append((mk_ref, 2 * p * LANES, mv_ref, (2 * p + a) * V_ROWS))
    n = len(chains)

    def tile_step(start, ms, nk=1, mask=None, first=False):
        keys = nk * tq
        tile_max = []
        for c, (k_ref, k_off, _, _) in enumerate(chains):
            s_t = _dot(k_ref[0, pl.ds(start, keys), k_off:k_off + 2 * LANES], qt_ref[c])
            if mask is not None:
                s_t = jnp.where(mask, s_t, -jnp.inf)
            s_ref[c, 0:keys, :] = s_t
            tile_max.append(jnp.max(s_t, axis=0, keepdims=True))
        new_ms = []
        for c, (m, (_, _, v_ref, v_row)) in enumerate(zip(ms, chains)):
            m_new = jnp.maximum(m, tile_max[c])
            new_ms.append(m_new)
            p_t = jnp.exp2(s_ref[c, 0:keys, :] - m_new).astype(BF16)
            pv = _dot(v_ref[0, v_row:v_row + V_ROWS, pl.ds(start, keys)], p_t)
            acc_ref[c] = pv if first else jnp.exp2(m - m_new) * acc_ref[c] + pv
        return new_ms

    key = lax.broadcasted_iota(jnp.int32, (tq, tq), 0)
    qry = lax.broadcasted_iota(jnp.int32, (tq, tq), 1)
    ms = tile_step(pl.multiple_of(qi * tq, tq), [jnp.full((1, tq), -jnp.inf, F32)] * n,
                   mask=key <= qry, first=True)
    ms = lax.fori_loop(0, qi // 2, lambda j, ms: tile_step(pl.multiple_of(j * 2 * tq, 2 * tq), ms, nk=2), ms)

    @pl.when(qi % 2 == 1)
    def _():
        tile_step(pl.multiple_of((qi - 1) * tq, tq), ms)

    for kind, g_ref in enumerate((fg_ref, mg_ref)):
        for p in range(N_PAIRS):
            o = []
            for c in (kind * N_HEADS + 2 * p, kind * N_HEADS + 2 * p + 1):
                o.append(acc_ref[c, 0:HEAD_DIM, :] / acc_ref[c, HEAD_DIM:HEAD_DIM + 1, :])
            y = jnp.concatenate(o, axis=0).T
            lanes = slice(p * LANES, (p + 1) * LANES)
            y_ref[0, kind, :, lanes] = (y * g_ref[0, :, lanes].astype(F32)).astype(BF16)


def _attn_core(fq, fk, fv, fg, mq, mk, mv, mg):
    B, S, _ = fk.shape
    tq = MOBA_BLOCK
    n_chains = 2 * N_HEADS
    q_tile = lambda width: pl.BlockSpec((1, tq, width), lambda b, i: (b, i, 0))
    qt_tile = lambda rows: pl.BlockSpec((1, rows, tq), lambda b, i: (b, 0, i))
    k_full = pl.BlockSpec((1, S, 2 * HEAD_WIDTH), lambda b, i: (b, 0, 0))
    v_full = pl.BlockSpec((1, N_HEADS * V_ROWS, S), lambda b, i: (b, 0, 0))
    return pl.pallas_call(
        functools.partial(_attn_core_kernel, seq=S),
        grid=(B, S // tq),
        in_specs=[qt_tile(2 * HEAD_WIDTH), k_full, v_full, q_tile(HEAD_WIDTH),
                  qt_tile(HEAD_WIDTH), k_full, v_full, q_tile(HEAD_WIDTH)],
        out_specs=pl.BlockSpec((1, 2, tq, HEAD_WIDTH), lambda b, i: (b, 0, i, 0)),
        out_shape=jax.ShapeDtypeStruct((B, 2, S, HEAD_WIDTH), BF16),
        scratch_shapes=[pltpu.VMEM((N_PAIRS, LANES, LANES), F32),
                        pltpu.VMEM((n_chains, 2 * LANES, tq), BF16),
                        pltpu.VMEM((n_chains, 2 * tq, tq), F32),
                        pltpu.VMEM((n_chains, V_ROWS, tq), F32)],
        compiler_params=pltpu.CompilerParams(
            dimension_semantics=("arbitrary", "arbitrary"), vmem_limit_bytes=VMEM_LIMIT),
        name="attn_core",
    )(fq, fk, fv, fg, mq, mk, mv, mg)


def _pool_kernel(y_ref, x_ref, wa_ref, ga_ref, ba_ref, win_ref, wout_ref, g_ref, b_ref, o_ref, ubuf_ref,
                 *, tm, sub):
    s = pl.program_id(1)

    @pl.when(s == 0)
    def _():
        ubuf_ref[0:POOL_HALO, :] = jnp.zeros((POOL_HALO, POOL_WIDTH), F32)

    @pl.when(s > 0)
    def _():
        ubuf_ref[0:POOL_HALO, :] = ubuf_ref[tm:tm + POOL_HALO, :]

    n_groups = len(POOL_WINDOWS)

    def attn_out(r0):
        rows = slice(r0, r0 + sub)
        fa = _dot(y_ref[0, 0, rows, :], wa_ref[0]) + _dot(y_ref[0, 1, rows, :], wa_ref[1])
        return _layer_norm(DEEPNORM_ALPHA * x_ref[0, rows, :] + fa, ga_ref[...], ba_ref[...])

    x_next = attn_out(0)
    for r0 in range(0, tm, sub):
        x = x_next
        if r0 + sub < tm:
            x_next = attn_out(r0 + sub)
        xb = x.astype(BF16)
        t = s * tm + r0 + lax.broadcasted_iota(jnp.int32, (sub, LANES), 0)

        def project(g):
            cols = slice(g * POOL_GROUP, (g + 1) * POOL_GROUP)
            gcols = slice(POOL_WIDTH + g * POOL_GROUP, POOL_WIDTH + (g + 1) * POOL_GROUP)
            return _dot(xb, win_ref[:, cols]), _dot(xb, win_ref[:, gcols])

        def mix(g, u, gate):
            w = POOL_WINDOWS[g]
            cols = slice(g * POOL_GROUP, (g + 1) * POOL_GROUP)
            ubuf_ref[POOL_HALO + r0:POOL_HALO + r0 + sub, cols] = u
            win = ubuf_ref[r0:r0 + POOL_HALO + sub, cols]
            span = 1
            while span < w:
                win = win + pltpu.roll(win, span, 0)
                span *= 2
            inv = 1.0 / jnp.minimum(t + 1, w).astype(F32)
            inv = jnp.concatenate([inv] * (POOL_GROUP // LANES), axis=-1)
            pooled = win[POOL_HALO:, :] * inv - u
            return _dot((pooled * _silu(gate)).astype(BF16), wout_ref[cols, :])

        ahead = project(0)
        f = jnp.zeros((sub, D_MODEL), F32)
        for g in range(n_groups):
            cur = ahead
            if g + 1 < n_groups:
                ahead = project(g + 1)
            f = f + mix(g, *cur)
        z = DEEPNORM_ALPHA * x + f
        o_ref[0, r0:r0 + sub, :] = _layer_norm(z, g_ref[...], b_ref[...])


def _pool_layer(y, x, w_attn_out, w_in, w_out, ln_g, ln_b, pool_layer, layer, *, tm=512, sub=256):
    B, S, D = x.shape
    full = lambda shape: pl.BlockSpec(shape, lambda b, s: (0,) * len(shape), pipeline_mode=pl.Buffered(1))
    tile = pl.BlockSpec((1, tm, D), lambda b, s: (b, s, 0))
    return pl.pallas_call(
        functools.partial(_pool_kernel, tm=tm, sub=sub),
        grid=(B, S // tm),
        in_specs=[pl.BlockSpec((1, 2, tm, HEAD_WIDTH), lambda b, s: (b, 0, s, 0)), tile,
                  _layer_slab(w_attn_out, pool_layer), _layer_slab(ln_g, layer - 1), _layer_slab(ln_b, layer - 1),
                  full(w_in.shape), _layer_slab(w_out, pool_layer),
                  _layer_slab(ln_g, layer), _layer_slab(ln_b, layer)],
        out_specs=tile,
        out_shape=jax.ShapeDtypeStruct((B, S, D), F32),
        scratch_shapes=[pltpu.VMEM((POOL_HALO + tm, POOL_WIDTH), F32)],
        compiler_params=pltpu.CompilerParams(
            dimension_semantics=("arbitrary", "arbitrary"), vmem_limit_bytes=VMEM_LIMIT),
        name="pool_layer",
    )(y, x, w_attn_out, ln_g, ln_b, w_in, w_out, ln_g, ln_b)


def _split2(v):
    hi = v.astype(BF16)
    return hi, (v - hi.astype(F32)).astype(BF16)


def _pool_fold_kernel(win_ref, wgrp_ref, scale_ref, o_ref):
    i = pl.program_id(0)
    n_groups = len(POOL_WINDOWS)

    @pl.when(i < n_groups)
    def _():
        a_hi, a_lo = _split2(win_ref[...])
        b_hi, b_lo = _split2(wgrp_ref[0])
        prod = (_dot(a_hi, b_hi) + _dot(a_hi, b_lo)) + _dot(a_lo, b_hi)
        o_ref[...] = (prod * scale_ref[...]).astype(BF16)

    @pl.when(i >= n_groups)
    def _():
        o_ref[...] = win_ref[...].astype(BF16)


def _pool_fold_weights(w_in, w_grp, scale, pool_layer):
    n_groups = len(POOL_WINDOWS)
    _, D, width = w_in.shape
    group = lambda i: jnp.minimum(i, n_groups - 1)
    return pl.pallas_call(
        _pool_fold_kernel,
        grid=(width // POOL_GROUP,),
        in_specs=[pl.BlockSpec((None, D, POOL_GROUP), lambda i: (pool_layer, 0, i)),
                  pl.BlockSpec((None, 1, POOL_GROUP, POOL_GROUP), lambda i: (pool_layer, group(i), 0, 0)),
                  pl.BlockSpec((None, 1, POOL_GROUP), lambda i: (pool_layer, 0, group(i)))],
        out_specs=pl.BlockSpec((D, POOL_GROUP), lambda i: (0, i)),
        out_shape=jax.ShapeDtypeStruct((D, width), BF16),
        compiler_params=pltpu.CompilerParams(dimension_semantics=("arbitrary",), vmem_limit_bytes=VMEM_LIMIT),
        name="pool_fold_weights",
    )(w_in, w_grp, scale)


def kernel(x, attn_w_in, attn_b_f, attn_w_out, pool_w_in, pool_w_grp, pool_scale, pool_w_out, ln_g, ln_b):
    W = HEAD_WIDTH
    n_ff = N_HEADS
    w_main = _attn_prep_weights(attn_w_in)
    b_pad = jnp.pad(attn_b_f, ((0, 0), (0, LANES - n_ff)))[:, None, :]
    w_attn_out = attn_w_out.reshape(-1, 2, W, D_MODEL).astype(BF16)
    w_pool_out = pool_w_out.astype(BF16)
    scale = pool_scale[:, None, :]
    ln_g = ln_g[:, None, :]
    ln_b = ln_b[:, None, :]

    assert DEPTH % 2 == 0
    for layer in range(0, DEPTH, 2):
        j = layer // 2
        y = _attn_core(*_attn_in_proj(x, w_main, attn_w_in, b_pad, j))
        w_in = _pool_fold_weights(pool_w_in, pool_w_grp, scale, j)
        x = _pool_layer(y, x, w_attn_out, w_in, w_pool_out, ln_g, ln_b, j, layer + 1)
    return x
```

```python
import functools
import math

import jax
import jax.numpy as jnp
import numpy as np
from jax import lax
from jax.experimental import pallas as pl
from jax.experimental.pallas import tpu as pltpu

D_MODEL = 1024
DEPTH = 4
HEAD_DIM = 64
N_HEADS = 8
HEAD_WIDTH = N_HEADS * HEAD_DIM
N_PAIRS = N_HEADS // 2
MOBA_BLOCK = 256
MOBA_TOPK = 3
POOL_WIDTH = 2048
POOL_WINDOWS = (2, 4, 8, 16)
POOL_GROUP = 512
POOL_HALO = 16
DEEPNORM_ALPHA = (2 * DEPTH) ** 0.25
LN_EPS = 1e-5
LOG2E = math.log2(math.e)
Q_SCALE = HEAD_DIM ** -0.5 * LOG2E

LANES = 128
SUBLANES = 8
BF16_SUBLANES = 2 * SUBLANES
V_ROWS = HEAD_DIM + BF16_SUBLANES
MASK_BIAS = -1e30
VMEM_LIMIT = 56 * 1024 * 1024

BF16 = jnp.bfloat16
F32 = jnp.float32


def _dot(a, b):
    return jnp.dot(a, b, preferred_element_type=F32)


def _split3(v):
    hi = v.astype(BF16)
    r1 = v - hi.astype(F32)
    mid = r1.astype(BF16)
    lo = (r1 - mid.astype(F32)).astype(BF16)
    return hi, mid, lo


def _layer_norm(z, g, b):
    mu = jnp.mean(z, axis=-1, keepdims=True)
    zc = z - mu
    var = jnp.mean(zc * zc, axis=-1, keepdims=True)
    return zc * lax.rsqrt(var + LN_EPS) * g + b


def _silu(v):
    return v * jax.nn.sigmoid(v)


def _pack3(v):
    lane = lax.broadcasted_iota(jnp.int32, v.shape, 1)
    hi, mid, lo = (part.astype(F32) for part in _split3(v))
    packed = jnp.where(lane < N_HEADS, hi,
                       jnp.where(lane < 2 * N_HEADS, pltpu.roll(mid, N_HEADS, 1),
                                 jnp.where(lane < 3 * N_HEADS, pltpu.roll(lo, 2 * N_HEADS, 1), 0.0)))
    return packed.astype(BF16)


def _attn_in_kernel(x_ref, w_ref, wff_ref, bf_ref, place_ref, const_ref,
                    fq_ref, fk_ref, fv_ref, fg_ref, mq_ref, mk_ref, mv_ref, mg_ref,
                    carry_ref, *, tm, sub):
    s = pl.program_id(1)

    @pl.when(s == 0)
    def _():
        carry_ref[...] = jnp.zeros_like(carry_ref)

    weights = (w_ref, wff_ref, bf_ref, place_ref, const_ref)
    outs = (fq_ref, fk_ref, fv_ref, fg_ref, mq_ref, mk_ref, mv_ref, mg_ref)
    for r0 in range(0, tm, sub):
        _attn_in_subtile(x_ref, weights, outs, carry_ref, s * tm + r0, slice(r0, r0 + sub))


def _attn_in_subtile(x_ref, weights, outs, carry_ref, t0, rows):
    w_ref, wff_ref, bf_ref, place_ref, const_ref = weights
    fq_ref, fk_ref, fv_ref, fg_ref, mq_ref, mk_ref, mv_ref, mg_ref = outs
    tm = rows.stop - rows.start
    xb = x_ref[0, rows, :].astype(BF16)
    W = HEAD_WIDTH

    def project(i):
        return _dot(xb, w_ref[:, i * W:(i + 1) * W])

    blk = (t0 + lax.broadcasted_iota(jnp.int32, (tm, LANES), 0)) // MOBA_BLOCK
    onehot = jnp.where(lax.broadcasted_iota(jnp.int32, (tm, LANES), 1) == blk, 1.0, 0.0).astype(BF16)

    def pair_lanes(p):
        return slice(2 * p * LANES, (2 * p + 1) * LANES), slice((2 * p + 1) * LANES, (2 * p + 2) * LANES)

    def store_values(v_ref, h):
        ones = jnp.ones((V_ROWS - HEAD_DIM, tm), BF16)
        for p in range(N_PAIRS):
            vt = h[:, p * LANES:(p + 1) * LANES].T.astype(BF16)
            for a in range(2):
                h0 = (2 * p + a) * V_ROWS
                v_ref[0, h0:h0 + HEAD_DIM, rows] = vt[a * HEAD_DIM:(a + 1) * HEAD_DIM, :]
                v_ref[0, h0 + HEAD_DIM:h0 + V_ROWS, rows] = ones

    head_lanes = lax.broadcasted_iota(jnp.int32, wff_ref.shape, 1) < N_HEADS
    w_ff = jnp.where(head_lanes, wff_ref[...], 0.0).astype(BF16)
    ff = _dot(xb, w_ff) + bf_ref[...]
    log_f_parts = _pack3(jax.nn.log_sigmoid(ff))
    store_values(fv_ref, project(2))
    fg_ref[0, rows, :] = _silu(project(3)).astype(BF16)

    row = lax.broadcasted_iota(jnp.int32, (tm, tm), 0)
    col = lax.broadcasted_iota(jnp.int32, (tm, tm), 1)
    tri = jnp.where(col <= row, 1.0, 0.0).astype(BF16)
    parts = _dot(tri, log_f_parts)
    c = (parts + pltpu.roll(parts, LANES - N_HEADS, 1)) + pltpu.roll(parts, LANES - 2 * N_HEADS, 1)
    c = c + carry_ref[0:1, :]
    carry_ref[0:1, :] = c[tm - 1:tm, :]
    c_parts = _pack3(c * LOG2E)
    h = project(4)
    for p in range(N_PAIRS):
        mq_ref[0, p * LANES:(p + 1) * LANES, rows] = h[:, p * LANES:(p + 1) * LANES].T.astype(BF16)
    h = project(5)
    for p in range(N_PAIRS):
        lo_lanes, hi_lanes = pair_lanes(p)
        mk_ref[0, rows, lo_lanes] = h[:, p * LANES:(p + 1) * LANES].astype(BF16)
        mk_ref[0, rows, hi_lanes] = onehot
    ext = _dot(c_parts, place_ref[...]) + const_ref[...]

    store_values(mv_ref, project(6))
    mg_ref[0, rows, :] = _silu(project(7)).astype(BF16)
    h = project(0)
    for p in range(N_PAIRS):
        lo_lanes, hi_lanes = pair_lanes(p)
        fq_ref[0, lo_lanes, rows] = h[:, p * LANES:(p + 1) * LANES].T.astype(BF16)
        fq_ref[0, hi_lanes, rows] = ext[:, p * LANES:(p + 1) * LANES].T.astype(BF16)
    h = project(1)
    for p in range(N_PAIRS):
        lo_lanes, hi_lanes = pair_lanes(p)
        fk_ref[0, rows, lo_lanes] = h[:, p * LANES:(p + 1) * LANES].astype(BF16)
        fk_ref[0, rows, hi_lanes] = ext[:, W + p * LANES:W + (p + 1) * LANES].astype(BF16)


def _decay_placement():
    place = np.zeros((LANES, 2 * HEAD_WIDTH), np.float32)
    const = np.zeros((1, 2 * HEAD_WIDTH), np.float32)
    for h in range(N_HEADS):
        p, a = divmod(h, 2)
        base_q = p * LANES + a * HEAD_DIM
        base_k = HEAD_WIDTH + p * LANES + a * HEAD_DIM
        for r in range(3):
            place[r * N_HEADS + h, base_q + r] = 1.0
            const[0, base_q + 3 + r] = 1.0
            const[0, base_k + r] = 1.0
            place[r * N_HEADS + h, base_k + 3 + r] = -1.0
    return jnp.asarray(place, BF16), jnp.asarray(const, F32)


KIND_GROUPS = 4


def _attn_prep_kernel(a_ref, b_ref, o_ref):
    g = pl.program_id(1)
    scale = jnp.where((g == 0) | (g == KIND_GROUPS), Q_SCALE, 1.0)

    @pl.when(g < KIND_GROUPS)
    def _():
        o_ref[...] = (a_ref[...] * scale).astype(BF16)

    @pl.when(g >= KIND_GROUPS)
    def _():
        width = a_ref.shape[1]
        lane = lax.broadcasted_iota(jnp.int32, a_ref.shape, 1)
        shifted = jnp.where(lane < width - N_HEADS, pltpu.roll(a_ref[...], width - N_HEADS, 1),
                            pltpu.roll(b_ref[...], width - N_HEADS, 1))
        o_ref[...] = (shifted * scale).astype(BF16)


def _attn_prep_weights(w_in):
    L, D, _ = w_in.shape
    W = HEAD_WIDTH
    return pl.pallas_call(
        _attn_prep_kernel,
        grid=(L, 2 * KIND_GROUPS),
        in_specs=[pl.BlockSpec((None, D, W), lambda l, g: (l, 0, g)),
                  pl.BlockSpec((None, D, W), lambda l, g: (l, 0, jnp.where(g >= KIND_GROUPS, g + 1, g)))],
        out_specs=pl.BlockSpec((None, D, W), lambda l, g: (l, 0, g)),
        out_shape=jax.ShapeDtypeStruct((L, D, 2 * KIND_GROUPS * W), BF16),
        compiler_params=pltpu.CompilerParams(
            dimension_semantics=("arbitrary", "arbitrary"), vmem_limit_bytes=VMEM_LIMIT),
        name="attn_prep_weights",
    )(w_in, w_in)


def _layer_slab(stacked, layer):
    tail = stacked.shape[1:]
    return pl.BlockSpec((None,) + tail, lambda *_: (layer,) + (0,) * len(tail), pipeline_mode=pl.Buffered(1))


def _attn_in_proj(x, w_main, w_in_raw, b_f, layer, *, tm=512, sub=256):
    B, S, D = x.shape
    place, const = _decay_placement()
    W = HEAD_WIDTH
    ff_block = pl.BlockSpec((None, D, LANES), lambda b, s: (layer, 0, 4 * W // LANES),
                            pipeline_mode=pl.Buffered(1))
    full = lambda shape: pl.BlockSpec(shape, lambda b, s: (0,) * len(shape))
    tile = lambda width: pl.BlockSpec((1, tm, width), lambda b, s: (b, s, 0))
    row_major = lambda w: (tile(w), jax.ShapeDtypeStruct((B, S, w), BF16))
    transposed = lambda rows: (pl.BlockSpec((1, rows, tm), lambda b, s: (b, 0, s)),
                               jax.ShapeDtypeStruct((B, rows, S), BF16))
    values = transposed(N_HEADS * V_ROWS)
    outs = (transposed(2 * W), row_major(2 * W), values, row_major(W),
            transposed(W), row_major(2 * W), values, row_major(W))
    return pl.pallas_call(
        functools.partial(_attn_in_kernel, tm=tm, sub=sub),
        grid=(B, S // tm),
        in_specs=[tile(D), _layer_slab(w_main, layer), ff_block, _layer_slab(b_f, layer),
                  full(place.shape), full(const.shape)],
        out_specs=[o[0] for o in outs],
        out_shape=[o[1] for o in outs],
        scratch_shapes=[pltpu.VMEM((8, LANES), F32)],
        compiler_params=pltpu.CompilerParams(
            dimension_semantics=("arbitrary", "arbitrary"), vmem_limit_bytes=VMEM_LIMIT),
        name="attn_in_proj",
    )(x, w_main, w_in_raw, b_f, place, const)


def _head_row_mask(shape, a):
    row = lax.broadcasted_iota(jnp.int32, shape, 0)
    return (row // HEAD_DIM) % 2 == a


def _moba_bias_t(qa_t, kmean, qi, nb):
    tq = qa_t.shape[1]
    gate = _dot(kmean, qa_t)[0:SUBLANES, :]
    blk = lax.broadcasted_iota(jnp.int32, gate.shape, 0)
    gate = jnp.where(blk < qi, gate, -jnp.inf)
    rank = jnp.zeros(gate.shape, jnp.int32)
    for d in range(1, nb):
        other = pltpu.roll(gate, d, 0)
        rank = rank + jnp.where(blk >= d, jnp.where(other >= gate, 1, 0), jnp.where(other > gate, 1, 0))
    keep = ((rank < MOBA_TOPK) & (gate > -jnp.inf)) | (blk == qi)
    bias_t = jnp.where(keep, 0.0, MASK_BIAS)
    return jnp.concatenate([bias_t, jnp.zeros((LANES - SUBLANES, tq), F32)], axis=0).astype(BF16)


def _attn_core_kernel(fq_ref, fk_ref, fv_ref, fg_ref, mq_ref, mk_ref, mv_ref, mg_ref, y_ref,
                      kmean_ref, qt_ref, s_ref, acc_ref, *, seq):
    tq = MOBA_BLOCK
    nb = seq // tq
    assert nb == SUBLANES
    qi = pl.program_id(1)

    @pl.when(qi == 0)
    def _():
        r = lax.broadcasted_iota(jnp.int32, (LANES, seq), 0)
        t = lax.broadcasted_iota(jnp.int32, (LANES, seq), 1)
        avg = jnp.where(t // tq == r, 1.0 / tq, 0.0).astype(BF16)
        for p in range(N_PAIRS):
            kmean_ref[p] = _dot(avg, mk_ref[0, :, 2 * p * LANES:(2 * p + 1) * LANES])

    chains = []
    for p in range(N_PAIRS):
        fq_t = fq_ref[0, 2 * p * LANES:(2 * p + 2) * LANES, :]
        for a in range(2):
            qt_ref[len(chains)] = jnp.where(_head_row_mask(fq_t.shape, a), fq_t, jnp.zeros_like(fq_t))
            chains.append((fk_ref, 2 * p * LANES, fv_ref, (2 * p + a) * V_ROWS))
    for p in range(N_PAIRS):
        mq_t = mq_ref[0, p * LANES:(p + 1) * LANES, :]
        kmean = kmean_ref[p].astype(BF16)
        for a in range(2):
            qa_t = jnp.where(_head_row_mask(mq_t.shape, a), mq_t, jnp.zeros_like(mq_t))
            qt_ref[len(chains)] = jnp.concatenate([qa_t, _moba_bias_t(qa_t, kmean, qi, nb)], axis=0)
            chains.append((mk_ref, 2 * p * LANES, mv_ref, (2 * p + a) * V_ROWS))
    n = len(chains)

    def tile_step(start, ms, nk=1, mask=None, first=False):
        keys = nk * tq
        tile_max = []
        for c, (k_ref, k_off, _, _) in enumerate(chains):
            s_t = _dot(k_ref[0, pl.ds(start, keys), k_off:k_off + 2 * LANES], qt_ref[c])
            if mask is not None:
                s_t = jnp.where(mask, s_t, -jnp.inf)
            s_ref[c, 0:keys, :] = s_t
            tile_max.append(jnp.max(s_t, axis=0, keepdims=True))
        new_ms = []
        for c, (m, (_, _, v_ref, v_row)) in enumerate(zip(ms, chains)):
            m_new = jnp.maximum(m, tile_max[c])
            new_ms.append(m_new)
            p_t = jnp.exp2(s_ref[c, 0:keys, :] - m_new).astype(BF16)
            pv = _dot(v_ref[0, v_row:v_row + V_ROWS, pl.ds(start, keys)], p_t)
            acc_ref[c] = pv if first else jnp.exp2(m - m_new) * acc_ref[c] + pv
        return new_ms

    key = lax.broadcasted_iota(jnp.int32, (tq, tq), 0)
    qry = lax.broadcasted_iota(jnp.int32, (tq, tq), 1)
    ms = tile_step(pl.multiple_of(qi * tq, tq), [jnp.full((1, tq), -jnp.inf, F32)] * n,
                   mask=key <= qry, first=True)
    ms = lax.fori_loop(0, qi // 2, lambda j, ms: tile_step(pl.multiple_of(j * 2 * tq, 2 * tq), ms, nk=2), ms)

    @pl.when(qi % 2 == 1)
    def _():
        tile_step(pl.multiple_of((qi - 1) * tq, tq), ms)

    for kind, g_ref in enumerate((fg_ref, mg_ref)):
        for p in range(N_PAIRS):
            o = []
            for c in (kind * N_HEADS + 2 * p, kind * N_HEADS + 2 * p + 1):
                o.append(acc_ref[c, 0:HEAD_DIM, :] / acc_ref[c, HEAD_DIM:HEAD_DIM + 1, :])
            y = jnp.concatenate(o, axis=0).T
            lanes = slice(p * LANES, (p + 1) * LANES)
            y_ref[0, kind, :, lanes] = (y * g_ref[0, :, lanes].astype(F32)).astype(BF16)


def _attn_core(fq, fk, fv, fg, mq, mk, mv, mg):
    B, S, _ = fk.shape
    tq = MOBA_BLOCK
    n_chains = 2 * N_HEADS
    q_tile = lambda width: pl.BlockSpec((1, tq, width), lambda b, i: (b, i, 0))
    qt_tile = lambda rows: pl.BlockSpec((1, rows, tq), lambda b, i: (b, 0, i))
    k_full = pl.BlockSpec((1, S, 2 * HEAD_WIDTH), lambda b, i: (b, 0, 0))
    v_full = pl.BlockSpec((1, N_HEADS * V_ROWS, S), lambda b, i: (b, 0, 0))
    return pl.pallas_call(
        functools.partial(_attn_core_kernel, seq=S),
        grid=(B, S // tq),
        in_specs=[qt_tile(2 * HEAD_WIDTH), k_full, v_full, q_tile(HEAD_WIDTH),
                  qt_tile(HEAD_WIDTH), k_full, v_full, q_tile(HEAD_WIDTH)],
        out_specs=pl.BlockSpec((1, 2, tq, HEAD_WIDTH), lambda b, i: (b, 0, i, 0)),
        out_shape=jax.ShapeDtypeStruct((B, 2, S, HEAD_WIDTH), BF16),
        scratch_shapes=[pltpu.VMEM((N_PAIRS, LANES, LANES), F32),
                        pltpu.VMEM((n_chains, 2 * LANES, tq), BF16),
                        pltpu.VMEM((n_chains, 2 * tq, tq), F32),
                        pltpu.VMEM((n_chains, V_ROWS, tq), F32)],
        compiler_params=pltpu.CompilerParams(
            dimension_semantics=("arbitrary", "arbitrary"), vmem_limit_bytes=VMEM_LIMIT),
        name="attn_core",
    )(fq, fk, fv, fg, mq, mk, mv, mg)


def _pool_kernel(y_ref, x_ref, wa_ref, ga_ref, ba_ref, win_ref, wout_ref, g_ref, b_ref, o_ref, ubuf_ref,
                 *, tm, sub):
    s = pl.program_id(1)

    @pl.when(s == 0)
    def _():
        ubuf_ref[0:POOL_HALO, :] = jnp.zeros((POOL_HALO, POOL_WIDTH), F32)

    @pl.when(s > 0)
    def _():
        ubuf_ref[0:POOL_HALO, :] = ubuf_ref[tm:tm + POOL_HALO, :]

    n_groups = len(POOL_WINDOWS)

    def attn_out(r0):
        rows = slice(r0, r0 + sub)
        fa = _dot(y_ref[0, 0, rows, :], wa_ref[0]) + _dot(y_ref[0, 1, rows, :], wa_ref[1])
        return _layer_norm(DEEPNORM_ALPHA * x_ref[0, rows, :] + fa, ga_ref[...], ba_ref[...])

    x_next = attn_out(0)
    for r0 in range(0, tm, sub):
        x = x_next
        if r0 + sub < tm:
            x_next = attn_out(r0 + sub)
        xb = x.astype(BF16)
        t = s * tm + r0 + lax.broadcasted_iota(jnp.int32, (sub, LANES), 0)

        def project(g):
            cols = slice(g * POOL_GROUP, (g + 1) * POOL_GROUP)
            gcols = slice(POOL_WIDTH + g * POOL_GROUP, POOL_WIDTH + (g + 1) * POOL_GROUP)
            return _dot(xb, win_ref[:, cols]), _dot(xb, win_ref[:, gcols])

        def mix(g, u, gate):
            w = POOL_WINDOWS[g]
            cols = slice(g * POOL_GROUP, (g + 1) * POOL_GROUP)
            ubuf_ref[POOL_HALO + r0:POOL_HALO + r0 + sub, cols] = u
            win = ubuf_ref[r0:r0 + POOL_HALO + sub, cols]
            span = 1
            while span < w:
                win = win + pltpu.roll(win, span, 0)
                span *= 2
            inv = 1.0 / jnp.minimum(t + 1, w).astype(F32)
            inv = jnp.concatenate([inv] * (POOL_GROUP // LANES), axis=-1)
            pooled = win[POOL_HALO:, :] * inv - u
            return _dot((pooled * _silu(gate)).astype(BF16), wout_ref[cols, :])

        ahead = project(0)
        f = jnp.zeros((sub, D_MODEL), F32)
        for g in range(n_groups):
            cur = ahead
            if g + 1 < n_groups:
                ahead = project(g + 1)
            f = f + mix(g, *cur)
        z = DEEPNORM_ALPHA * x + f
        o_ref[0, r0:r0 + sub, :] = _layer_norm(z, g_ref[...], b_ref[...])


def _pool_layer(y, x, w_attn_out, w_in, w_out, ln_g, ln_b, pool_layer, layer, *, tm=512, sub=256):
    B, S, D = x.shape
    full = lambda shape: pl.BlockSpec(shape, lambda b, s: (0,) * len(shape), pipeline_mode=pl.Buffered(1))
    tile = pl.BlockSpec((1, tm, D), lambda b, s: (b, s, 0))
    return pl.pallas_call(
        functools.partial(_pool_kernel, tm=tm, sub=sub),
        grid=(B, S // tm),
        in_specs=[pl.BlockSpec((1, 2, tm, HEAD_WIDTH), lambda b, s: (b, 0, s, 0)), tile,
                  _layer_slab(w_attn_out, pool_layer), _layer_slab(ln_g, layer - 1), _layer_slab(ln_b, layer - 1),
                  full(w_in.shape), _layer_slab(w_out, pool_layer),
                  _layer_slab(ln_g, layer), _layer_slab(ln_b, layer)],
        out_specs=tile,
        out_shape=jax.ShapeDtypeStruct((B, S, D), F32),
        scratch_shapes=[pltpu.VMEM((POOL_HALO + tm, POOL_WIDTH), F32)],
        compiler_params=pltpu.CompilerParams(
            dimension_semantics=("arbitrary", "arbitrary"), vmem_limit_bytes=VMEM_LIMIT),
        name="pool_layer",
    )(y, x, w_attn_out, ln_g, ln_b, w_in, w_out, ln_g, ln_b)


def _split2(v):
    hi = v.astype(BF16)
    return hi, (v - hi.astype(F32)).astype(BF16)


def _pool_fold_kernel(win_ref, wgrp_ref, scale_ref, o_ref):
    i = pl.program_id(0)
    n_groups = len(POOL_WINDOWS)

    @pl.when(i < n_groups)
    def _():
        a_hi, a_lo = _split2(win_ref[...])
        b_hi, b_lo = _split2(wgrp_ref[0])
        prod = (_dot(a_hi, b_hi) + _dot(a_hi, b_lo)) + _dot(a_lo, b_hi)
        o_ref[...] = (prod * scale_ref[...]).astype(BF16)

    @pl.when(i >= n_groups)
    def _():
        o_ref[...] = win_ref[...].astype(BF16)


def _pool_fold_weights(w_in, w_grp, scale, pool_layer):
    n_groups = len(POOL_WINDOWS)
    _, D, width = w_in.shape
    group = lambda i: jnp.minimum(i, n_groups - 1)
    return pl.pallas_call(
        _pool_fold_kernel,
        grid=(width // POOL_GROUP,),
        in_specs=[pl.BlockSpec((None, D, POOL_GROUP), lambda i: (pool_layer, 0, i)),
                  pl.BlockSpec((None, 1, POOL_GROUP, POOL_GROUP), lambda i: (pool_layer, group(i), 0, 0)),
                  pl.BlockSpec((None, 1, POOL_GROUP), lambda i: (pool_layer, 0, group(i)))],
        out_specs=pl.BlockSpec((D, POOL_GROUP), lambda i: (0, i)),
        out_shape=jax.ShapeDtypeStruct((D, width), BF16),
        compiler_params=pltpu.CompilerParams(dimension_semantics=("arbitrary",), vmem_limit_bytes=VMEM_LIMIT),
        name="pool_fold_weights",
    )(w_in, w_grp, scale)


def kernel(x, attn_w_in, attn_b_f, attn_w_out, pool_w_in, pool_w_grp, pool_scale, pool_w_out, ln_g, ln_b):
    W = HEAD_WIDTH
    n_ff = N_HEADS
    w_main = _attn_prep_weights(attn_w_in)
    b_pad = jnp.pad(attn_b_f, ((0, 0), (0, LANES - n_ff)))[:, None, :]
    w_attn_out = attn_w_out.reshape(-1, 2, W, D_MODEL).astype(BF16)
    w_pool_out = pool_w_out.astype(BF16)
    scale = pool_scale[:, None, :]
    ln_g = ln_g[:, None, :]
    ln_b = ln_b[:, None, :]

    assert DEPTH % 2 == 0
    for layer in range(0, DEPTH, 2):
        j = layer // 2
        y = _attn_core(*_attn_in_proj(x, w_main, attn_w_in, b_pad, j))
        w_in = _pool_fold_weights(pool_w_in, pool_w_grp, scale, j)
        x = _pool_layer(y, x, w_attn_out, w_in, w_pool_out, ln_g, ln_b, j, layer + 1)
    return x
```

```python
import functools
import math

import jax
import jax.numpy as jnp
import numpy as np
from jax import lax
from jax.experimental import pallas as pl
from jax.experimental.pallas import tpu as pltpu

D_MODEL = 1024
DEPTH = 4
HEAD_DIM = 64
N_HEADS = 8
HEAD_WIDTH = N_HEADS * HEAD_DIM
N_PAIRS = N_HEADS // 2
MOBA_BLOCK = 256
MOBA_TOPK = 3
POOL_WIDTH = 2048
POOL_WINDOWS = (2, 4, 8, 16)
POOL_GROUP = 512
POOL_HALO = 16
DEEPNORM_ALPHA = (2 * DEPTH) ** 0.25
LN_EPS = 1e-5
LOG2E = math.log2(math.e)
Q_SCALE = HEAD_DIM ** -0.5 * LOG2E

LANES = 128
SUBLANES = 8
BF16_SUBLANES = 2 * SUBLANES
V_ROWS = HEAD_DIM + BF16_SUBLANES
MASK_BIAS = -1e30
VMEM_LIMIT = 56 * 1024 * 1024

BF16 = jnp.bfloat16
F32 = jnp.float32


def _dot(a, b):
    return jnp.dot(a, b, preferred_element_type=F32)


def _split3(v):
    hi = v.astype(BF16)
    r1 = v - hi.astype(F32)
    mid = r1.astype(BF16)
    lo = (r1 - mid.astype(F32)).astype(BF16)
    return hi, mid, lo


def _layer_norm(z, g, b):
    mu = jnp.mean(z, axis=-1, keepdims=True)
    zc = z - mu
    var = jnp.mean(zc * zc, axis=-1, keepdims=True)
    return zc * lax.rsqrt(var + LN_EPS) * g + b


def _silu(v):
    return v * jax.nn.sigmoid(v)


def _pack3(v):
    lane = lax.broadcasted_iota(jnp.int32, v.shape, 1)
    hi, mid, lo = (part.astype(F32) for part in _split3(v))
    packed = jnp.where(lane < N_HEADS, hi,
                       jnp.where(lane < 2 * N_HEADS, pltpu.roll(mid, N_HEADS, 1),
                                 jnp.where(lane < 3 * N_HEADS, pltpu.roll(lo, 2 * N_HEADS, 1), 0.0)))
    return packed.astype(BF16)


def _attn_in_kernel(x_ref, w_ref, wff_ref, bf_ref, place_ref, const_ref,
                    fq_ref, fk_ref, fv_ref, fg_ref, mq_ref, mk_ref, mv_ref, mg_ref,
                    carry_ref, *, tm, sub):
    s = pl.program_id(1)

    @pl.when(s == 0)
    def _():
        carry_ref[...] = jnp.zeros_like(carry_ref)

    weights = (w_ref, wff_ref, bf_ref, place_ref, const_ref)
    outs = (fq_ref, fk_ref, fv_ref, fg_ref, mq_ref, mk_ref, mv_ref, mg_ref)
    for r0 in range(0, tm, sub):
        _attn_in_subtile(x_ref, weights, outs, carry_ref, s * tm + r0, slice(r0, r0 + sub))


def _attn_in_subtile(x_ref, weights, outs, carry_ref, t0, rows):
    w_ref, wff_ref, bf_ref, place_ref, const_ref = weights
    fq_ref, fk_ref, fv_ref, fg_ref, mq_ref, mk_ref, mv_ref, mg_ref = outs
    tm = rows.stop - rows.start
    xb = x_ref[0, rows, :].astype(BF16)
    W = HEAD_WIDTH

    def project(i):
        return _dot(xb, w_ref[:, i * W:(i + 1) * W])

    blk = (t0 + lax.broadcasted_iota(jnp.int32, (tm, LANES), 0)) // MOBA_BLOCK
    onehot = jnp.where(lax.broadcasted_iota(jnp.int32, (tm, LANES), 1) == blk, 1.0, 0.0).astype(BF16)

    def pair_lanes(p):
        return slice(2 * p * LANES, (2 * p + 1) * LANES), slice((2 * p + 1) * LANES, (2 * p + 2) * LANES)

    def store_values(v_ref, h):
        ones = jnp.ones((V_ROWS - HEAD_DIM, tm), BF16)
        for p in range(N_PAIRS):
            vt = h[:, p * LANES:(p + 1) * LANES].T.astype(BF16)
            for a in range(2):
                h0 = (2 * p + a) * V_ROWS
                v_ref[0, h0:h0 + HEAD_DIM, rows] = vt[a * HEAD_DIM:(a + 1) * HEAD_DIM, :]
                v_ref[0, h0 + HEAD_DIM:h0 + V_ROWS, rows] = ones

    head_lanes = lax.broadcasted_iota(jnp.int32, wff_ref.shape, 1) < N_HEADS
    w_ff = jnp.where(head_lanes, wff_ref[...], 0.0).astype(BF16)
    ff = _dot(xb, w_ff) + bf_ref[...]
    log_f_parts = _pack3(jax.nn.log_sigmoid(ff))
    store_values(fv_ref, project(2))
    fg_ref[0, rows, :] = _silu(project(3)).astype(BF16)

    row = lax.broadcasted_iota(jnp.int32, (tm, tm), 0)
    col = lax.broadcasted_iota(jnp.int32, (tm, tm), 1)
    tri = jnp.where(col <= row, 1.0, 0.0).astype(BF16)
    parts = _dot(tri, log_f_parts)
    c = (parts + pltpu.roll(parts, LANES - N_HEADS, 1)) + pltpu.roll(parts, LANES - 2 * N_HEADS, 1)
    c = c + carry_ref[0:1, :]
    carry_ref[0:1, :] = c[tm - 1:tm, :]
    c_parts = _pack3(c * LOG2E)
    h = project(4)
    for p in range(N_PAIRS):
        mq_ref[0, p * LANES:(p + 1) * LANES, rows] = h[:, p * LANES:(p + 1) * LANES].T.astype(BF16)
    h = project(5)
    for p in range(N_PAIRS):
        lo_lanes, hi_lanes = pair_lanes(p)
        mk_ref[0, rows, lo_lanes] = h[:, p * LANES:(p + 1) * LANES].astype(BF16)
        mk_ref[0, rows, hi_lanes] = onehot
    ext = _dot(c_parts, place_ref[...]) + const_ref[...]

    store_values(mv_ref, project(6))
    mg_ref[0, rows, :] = _silu(project(7)).astype(BF16)
    h = project(0)
    for p in range(N_PAIRS):
        lo_lanes, hi_lanes = pair_lanes(p)
        fq_ref[0, lo_lanes, rows] = h[:, p * LANES:(p + 1) * LANES].T.astype(BF16)
        fq_ref[0, hi_lanes, rows] = ext[:, p * LANES:(p + 1) * LANES].T.astype(BF16)
    h = project(1)
    for p in range(N_PAIRS):
        lo_lanes, hi_lanes = pair_lanes(p)
        fk_ref[0, rows, lo_lanes] = h[:, p * LANES:(p + 1) * LANES].astype(BF16)
        fk_ref[0, rows, hi_lanes] = ext[:, W + p * LANES:W + (p + 1) * LANES].astype(BF16)


def _decay_placement():
    place = np.zeros((LANES, 2 * HEAD_WIDTH), np.float32)
    const = np.zeros((1, 2 * HEAD_WIDTH), np.float32)
    for h in range(N_HEADS):
        p, a = divmod(h, 2)
        base_q = p * LANES + a * HEAD_DIM
        base_k = HEAD_WIDTH + p * LANES + a * HEAD_DIM
        for r in range(3):
            place[r * N_HEADS + h, base_q + r] = 1.0
            const[0, base_q + 3 + r] = 1.0
            const[0, base_k + r] = 1.0
            place[r * N_HEADS + h, base_k + 3 + r] = -1.0
    return jnp.asarray(place, BF16), jnp.asarray(const, F32)


KIND_GROUPS = 4


def _attn_prep_kernel(a_ref, b_ref, o_ref):
    g = pl.program_id(1)
    scale = jnp.where((g == 0) | (g == KIND_GROUPS), Q_SCALE, 1.0)

    @pl.when(g < KIND_GROUPS)
    def _():
        o_ref[...] = (a_ref[...] * scale).astype(BF16)

    @pl.when(g >= KIND_GROUPS)
    def _():
        width = a_ref.shape[1]
        lane = lax.broadcasted_iota(jnp.int32, a_ref.shape, 1)
        shifted = jnp.where(lane < width - N_HEADS, pltpu.roll(a_ref[...], width - N_HEADS, 1),
                            pltpu.roll(b_ref[...], width - N_HEADS, 1))
        o_ref[...] = (shifted * scale).astype(BF16)


def _attn_prep_weights(w_in):
    L, D, _ = w_in.shape
    W = HEAD_WIDTH
    return pl.pallas_call(
        _attn_prep_kernel,
        grid=(L, 2 * KIND_GROUPS),
        in_specs=[pl.BlockSpec((None, D, W), lambda l, g: (l, 0, g)),
                  pl.BlockSpec((None, D, W), lambda l, g: (l, 0, jnp.where(g >= KIND_GROUPS, g + 1, g)))],
        out_specs=pl.BlockSpec((None, D, W), lambda l, g: (l, 0, g)),
        out_shape=jax.ShapeDtypeStruct((L, D, 2 * KIND_GROUPS * W), BF16),
        compiler_params=pltpu.CompilerParams(
            dimension_semantics=("arbitrary", "arbitrary"), vmem_limit_bytes=VMEM_LIMIT),
        name="attn_prep_weights",
    )(w_in, w_in)


def _layer_slab(stacked, layer):
    tail = stacked.shape[1:]
    return pl.BlockSpec((None,) + tail, lambda *_: (layer,) + (0,) * len(tail), pipeline_mode=pl.Buffered(1))


def _attn_in_proj(x, w_main, w_in_raw, b_f, layer, *, tm=512, sub=256):
    B, S, D = x.shape
    place, const = _decay_placement()
    W = HEAD_WIDTH
    ff_block = pl.BlockSpec((None, D, LANES), lambda b, s: (layer, 0, 4 * W // LANES),
                            pipeline_mode=pl.Buffered(1))
    full = lambda shape: pl.BlockSpec(shape, lambda b, s: (0,) * len(shape))
    tile = lambda width: pl.BlockSpec((1, tm, width), lambda b, s: (b, s, 0))
    row_major = lambda w: (tile(w), jax.ShapeDtypeStruct((B, S, w), BF16))
    transposed = lambda rows: (pl.BlockSpec((1, rows, tm), lambda b, s: (b, 0, s)),
                               jax.ShapeDtypeStruct((B, rows, S), BF16))
    values = transposed(N_HEADS * V_ROWS)
    outs = (transposed(2 * W), row_major(2 * W), values, row_major(W),
            transposed(W), row_major(2 * W), values, row_major(W))
    return pl.pallas_call(
        functools.partial(_attn_in_kernel, tm=tm, sub=sub),
        grid=(B, S // tm),
        in_specs=[tile(D), _layer_slab(w_main, layer), ff_block, _layer_slab(b_f, layer),
                  full(place.shape), full(const.shape)],
        out_specs=[o[0] for o in outs],
        out_shape=[o[1] for o in outs],
        scratch_shapes=[pltpu.VMEM((8, LANES), F32)],
        compiler_params=pltpu.CompilerParams(
            dimension_semantics=("arbitrary", "arbitrary"), vmem_limit_bytes=VMEM_LIMIT),
        name="attn_in_proj",
    )(x, w_main, w_in_raw, b_f, place, const)


def _head_row_mask(shape, a):
    row = lax.broadcasted_iota(jnp.int32, shape, 0)
    return (row // HEAD_DIM) % 2 == a


def _moba_bias_t(qa_t, kmean, qi, nb):
    tq = qa_t.shape[1]
    gate = _dot(kmean, qa_t)[0:SUBLANES, :]
    blk = lax.broadcasted_iota(jnp.int32, gate.shape, 0)
    gate = jnp.where(blk < qi, gate, -jnp.inf)
    rank = jnp.zeros(gate.shape, jnp.int32)
    for d in range(1, nb):
        other = pltpu.roll(gate, d, 0)
        rank = rank + jnp.where(blk >= d, jnp.where(other >= gate, 1, 0), jnp.where(other > gate, 1, 0))
    keep = ((rank < MOBA_TOPK) & (gate > -jnp.inf)) | (blk == qi)
    bias_t = jnp.where(keep, 0.0, MASK_BIAS)
    return jnp.concatenate([bias_t, jnp.zeros((LANES - SUBLANES, tq), F32)], axis=0).astype(BF16)


def _attn_core_kernel(fq_ref, fk_ref, fv_ref, fg_ref, mq_ref, mk_ref, mv_ref, mg_ref, y_ref,
                      kmean_ref, qt_ref, s_ref, acc_ref, *, seq):
    tq = MOBA_BLOCK
    nb = seq // tq
    assert nb == SUBLANES
    qi = pl.program_id(1)

    @pl.when(qi == 0)
    def _():
        r = lax.broadcasted_iota(jnp.int32, (LANES, seq), 0)
        t = lax.broadcasted_iota(jnp.int32, (LANES, seq), 1)
        avg = jnp.where(t // tq == r, 1.0 / tq, 0.0).astype(BF16)
        for p in range(N_PAIRS):
            kmean_ref[p] = _dot(avg, mk_ref[0, :, 2 * p * LANES:(2 * p + 1) * LANES])

    chains = []
    for p in range(N_PAIRS):
        fq_t = fq_ref[0, 2 * p * LANES:(2 * p + 2) * LANES, :]
        for a in range(2):
            qt_ref[len(chains)] = jnp.where(_head_row_mask(fq_t.shape, a), fq_t, jnp.zeros_like(fq_t))
            chains.append((fk_ref, 2 * p * LANES, fv_ref, (2 * p + a) * V_ROWS))
    for p in range(N_PAIRS):
        mq_t = mq_ref[0, p * LANES:(p + 1) * LANES, :]
        kmean = kmean_ref[p].astype(BF16)
        for a in range(2):
            qa_t = jnp.where(_head_row_mask(mq_t.shape, a), mq_t, jnp.zeros_like(mq_t))
            qt_ref[len(chains)] = jnp.concatenate([qa_t, _moba_bias_t(qa_t, kmean, qi, nb)], axis=0)
            chains.append((mk_ref, 2 * p * LANES, mv_ref, (2 * p + a) * V_ROWS))
    n = len(chains)

    def tile_step(start, ms, nk=1, mask=None, first=False):
        keys = nk * tq
        tile_max = [None] * n
        new_ms = [None] * n

        def scores(c):
            k_ref, k_off, _, _ = chains[c]
            s_t = _dot(k_ref[0, pl.ds(start, keys), k_off:k_off + 2 * LANES], qt_ref[c])
            if mask is not None:
                s_t = jnp.where(mask, s_t, -jnp.inf)
            s_ref[c, 0:keys, :] = s_t
            tile_max[c] = jnp.max(s_t, axis=0, keepdims=True)

        def update(c):
            _, _, v_ref, v_row = chains[c]
            m_new = jnp.maximum(ms[c], tile_max[c])
            new_ms[c] = m_new
            p_t = jnp.exp2(s_ref[c, 0:keys, :] - m_new).astype(BF16)
            pv = _dot(v_ref[0, v_row:v_row + V_ROWS, pl.ds(start, keys)], p_t)
            acc_ref[c] = pv if first else jnp.exp2(ms[c] - m_new) * acc_ref[c] + pv

        lead = n // 2
        for c in range(lead):
            scores(c)
        for c in range(n):
            update(c)
            if c + lead < n:
                scores(c + lead)
        return new_ms

    key = lax.broadcasted_iota(jnp.int32, (tq, tq), 0)
    qry = lax.broadcasted_iota(jnp.int32, (tq, tq), 1)
    ms = tile_step(pl.multiple_of(qi * tq, tq), [jnp.full((1, tq), -jnp.inf, F32)] * n,
                   mask=key <= qry, first=True)
    ms = lax.fori_loop(0, qi // 2, lambda j, ms: tile_step(pl.multiple_of(j * 2 * tq, 2 * tq), ms, nk=2), ms)

    @pl.when(qi % 2 == 1)
    def _():
        tile_step(pl.multiple_of((qi - 1) * tq, tq), ms)

    for kind, g_ref in enumerate((fg_ref, mg_ref)):
        for p in range(N_PAIRS):
            o = []
            for c in (kind * N_HEADS + 2 * p, kind * N_HEADS + 2 * p + 1):
                o.append(acc_ref[c, 0:HEAD_DIM, :] / acc_ref[c, HEAD_DIM:HEAD_DIM + 1, :])
            y = jnp.concatenate(o, axis=0).T
            lanes = slice(p * LANES, (p + 1) * LANES)
            y_ref[0, kind, :, lanes] = (y * g_ref[0, :, lanes].astype(F32)).astype(BF16)


def _attn_core(fq, fk, fv, fg, mq, mk, mv, mg):
    B, S, _ = fk.shape
    tq = MOBA_BLOCK
    n_chains = 2 * N_HEADS
    q_tile = lambda width: pl.BlockSpec((1, tq, width), lambda b, i: (b, i, 0))
    qt_tile = lambda rows: pl.BlockSpec((1, rows, tq), lambda b, i: (b, 0, i))
    k_full = pl.BlockSpec((1, S, 2 * HEAD_WIDTH), lambda b, i: (b, 0, 0))
    v_full = pl.BlockSpec((1, N_HEADS * V_ROWS, S), lambda b, i: (b, 0, 0))
    return pl.pallas_call(
        functools.partial(_attn_core_kernel, seq=S),
        grid=(B, S // tq),
        in_specs=[qt_tile(2 * HEAD_WIDTH), k_full, v_full, q_tile(HEAD_WIDTH),
                  qt_tile(HEAD_WIDTH), k_full, v_full, q_tile(HEAD_WIDTH)],
        out_specs=pl.BlockSpec((1, 2, tq, HEAD_WIDTH), lambda b, i: (b, 0, i, 0)),
        out_shape=jax.ShapeDtypeStruct((B, 2, S, HEAD_WIDTH), BF16),
        scratch_shapes=[pltpu.VMEM((N_PAIRS, LANES, LANES), F32),
                        pltpu.VMEM((n_chains, 2 * LANES, tq), BF16),
                        pltpu.VMEM((n_chains, 2 * tq, tq), F32),
                        pltpu.VMEM((n_chains, V_ROWS, tq), F32)],
        compiler_params=pltpu.CompilerParams(
            dimension_semantics=("arbitrary", "arbitrary"), vmem_limit_bytes=VMEM_LIMIT),
        name="attn_core",
    )(fq, fk, fv, fg, mq, mk, mv, mg)


def _pool_kernel(y_ref, x_ref, wa_ref, ga_ref, ba_ref, win_ref, wout_ref, g_ref, b_ref, o_ref, ubuf_ref,
                 *, tm, sub):
    s = pl.program_id(1)

    @pl.when(s == 0)
    def _():
        ubuf_ref[0:POOL_HALO, :] = jnp.zeros((POOL_HALO, POOL_WIDTH), F32)

    @pl.when(s > 0)
    def _():
        ubuf_ref[0:POOL_HALO, :] = ubuf_ref[tm:tm + POOL_HALO, :]

    n_groups = len(POOL_WINDOWS)

    def attn_out(r0):
        rows = slice(r0, r0 + sub)
        fa = _dot(y_ref[0, 0, rows, :], wa_ref[0]) + _dot(y_ref[0, 1, rows, :], wa_ref[1])
        return _layer_norm(DEEPNORM_ALPHA * x_ref[0, rows, :] + fa, ga_ref[...], ba_ref[...])

    x_next = attn_out(0)
    for r0 in range(0, tm, sub):
        x = x_next
        if r0 + sub < tm:
            x_next = attn_out(r0 + sub)
        xb = x.astype(BF16)
        t = s * tm + r0 + lax.broadcasted_iota(jnp.int32, (sub, LANES), 0)

        def project(g):
            cols = slice(g * POOL_GROUP, (g + 1) * POOL_GROUP)
            gcols = slice(POOL_WIDTH + g * POOL_GROUP, POOL_WIDTH + (g + 1) * POOL_GROUP)
            return _dot(xb, win_ref[:, cols]), _dot(xb, win_ref[:, gcols])

        def mix(g, u, gate):
            w = POOL_WINDOWS[g]
            cols = slice(g * POOL_GROUP, (g + 1) * POOL_GROUP)
            ubuf_ref[POOL_HALO + r0:POOL_HALO + r0 + sub, cols] = u
            win = ubuf_ref[r0:r0 + POOL_HALO + sub, cols]
            span = 1
            while span < w:
                win = win + pltpu.roll(win, span, 0)
                span *= 2
            inv = 1.0 / jnp.minimum(t + 1, w).astype(F32)
            inv = jnp.concatenate([inv] * (POOL_GROUP // LANES), axis=-1)
            pooled = win[POOL_HALO:, :] * inv - u
            return _dot((pooled * _silu(gate)).astype(BF16), wout_ref[cols, :])

        ahead = project(0)
        f = jnp.zeros((sub, D_MODEL), F32)
        for g in range(n_groups):
            cur = ahead
            if g + 1 < n_groups:
                ahead = project(g + 1)
            f = f + mix(g, *cur)
        z = DEEPNORM_ALPHA * x + f
        o_ref[0, r0:r0 + sub, :] = _layer_norm(z, g_ref[...], b_ref[...])


def _pool_layer(y, x, w_attn_out, w_in, w_out, ln_g, ln_b, pool_layer, layer, *, tm=512, sub=256):
    B, S, D = x.shape
    full = lambda shape: pl.BlockSpec(shape, lambda b, s: (0,) * len(shape), pipeline_mode=pl.Buffered(1))
    tile = pl.BlockSpec((1, tm, D), lambda b, s: (b, s, 0))
    return pl.pallas_call(
        functools.partial(_pool_kernel, tm=tm, sub=sub),
        grid=(B, S // tm),
        in_specs=[pl.BlockSpec((1, 2, tm, HEAD_WIDTH), lambda b, s: (b, 0, s, 0)), tile,
                  _layer_slab(w_attn_out, pool_layer), _layer_slab(ln_g, layer - 1), _layer_slab(ln_b, layer - 1),
                  full(w_in.shape), _layer_slab(w_out, pool_layer),
                  _layer_slab(ln_g, layer), _layer_slab(ln_b, layer)],
        out_specs=tile,
        out_shape=jax.ShapeDtypeStruct((B, S, D), F32),
        scratch_shapes=[pltpu.VMEM((POOL_HALO + tm, POOL_WIDTH), F32)],
        compiler_params=pltpu.CompilerParams(
            dimension_semantics=("arbitrary", "arbitrary"), vmem_limit_bytes=VMEM_LIMIT),
        name="pool_layer",
    )(y, x, w_attn_out, ln_g, ln_b, w_in, w_out, ln_g, ln_b)


def _split2(v):
    hi = v.astype(BF16)
    return hi, (v - hi.astype(F32)).astype(BF16)


def _pool_fold_kernel(win_ref, wgrp_ref, scale_ref, o_ref):
    i = pl.program_id(0)
    n_groups = len(POOL_WINDOWS)

    @pl.when(i < n_groups)
    def _():
        a_hi, a_lo = _split2(win_ref[...])
        b_hi, b_lo = _split2(wgrp_ref[0])
        prod = (_dot(a_hi, b_hi) + _dot(a_hi, b_lo)) + _dot(a_lo, b_hi)
        o_ref[...] = (prod * scale_ref[...]).astype(BF16)

    @pl.when(i >= n_groups)
    def _():
        o_ref[...] = win_ref[...].astype(BF16)


def _pool_fold_weights(w_in, w_grp, scale, pool_layer):
    n_groups = len(POOL_WINDOWS)
    _, D, width = w_in.shape
    group = lambda i: jnp.minimum(i, n_groups - 1)
    return pl.pallas_call(
        _pool_fold_kernel,
        grid=(width // POOL_GROUP,),
        in_specs=[pl.BlockSpec((None, D, POOL_GROUP), lambda i: (pool_layer, 0, i)),
                  pl.BlockSpec((None, 1, POOL_GROUP, POOL_GROUP), lambda i: (pool_layer, group(i), 0, 0)),
                  pl.BlockSpec((None, 1, POOL_GROUP), lambda i: (pool_layer, 0, group(i)))],
        out_specs=pl.BlockSpec((D, POOL_GROUP), lambda i: (0, i)),
        out_shape=jax.ShapeDtypeStruct((D, width), BF16),
        compiler_params=pltpu.CompilerParams(dimension_semantics=("arbitrary",), vmem_limit_bytes=VMEM_LIMIT),
        name="pool_fold_weights",
    )(w_in, w_grp, scale)


def kernel(x, attn_w_in, attn_b_f, attn_w_out, pool_w_in, pool_w_grp, pool_scale, pool_w_out, ln_g, ln_b):
    W = HEAD_WIDTH
    n_ff = N_HEADS
    w_main = _attn_prep_weights(attn_w_in)
    b_pad = jnp.pad(attn_b_f, ((0, 0), (0, LANES - n_ff)))[:, None, :]
    w_attn_out = attn_w_out.reshape(-1, 2, W, D_MODEL).astype(BF16)
    w_pool_out = pool_w_out.astype(BF16)
    scale = pool_scale[:, None, :]
    ln_g = ln_g[:, None, :]
    ln_b = ln_b[:, None, :]

    assert DEPTH % 2 == 0
    for layer in range(0, DEPTH, 2):
        j = layer // 2
        y = _attn_core(*_attn_in_proj(x, w_main, attn_w_in, b_pad, j))
        w_in = _pool_fold_weights(pool_w_in, pool_w_grp, scale, j)
        x = _pool_layer(y, x, w_attn_out, w_in, w_pool_out, ln_g, ln_b, j, layer + 1)
    return x
```
